```python
import jax, jax.numpy as jnp
from jax import lax
import numpy as np

D_MODEL = 1024
BATCH = 4
SEQ = 8192
DEPTH = 1

CHUNK = 64
N_MEM = 256
ATT_HEADS = 8
HEAD_DIM = 64
D_ATT = ATT_HEADS * HEAD_DIM
D_LRU = D_MODEL - D_ATT
LRU_BLOCKS = 8
LRU_BLOCK = D_LRU // LRU_BLOCKS
CONV_W = 4
LRU_C = 8.0
LEFT_CHUNKS = 8
BAND = (LEFT_CHUNKS + 1) * CHUNK
MAX_REL = 128
X_HEADS = 4
X_HEAD_DIM = D_MODEL // X_HEADS
D_FF = -(-8 * D_MODEL // (3 * 256)) * 256
D_IN = 3 * D_ATT + 2 * D_LRU
EPS = 1e-6

kernel_name = "hymba_chunk_attn_rglru_xmem_swiglu"


def rmsnorm(x, g):
    x32 = x.astype(jnp.float32)
    y = x32 * lax.rsqrt(jnp.mean(x32 * x32, axis=-1, keepdims=True) + EPS)
    return (y * g.astype(jnp.float32)).astype(x.dtype)


def chunk_attention(q, k, v, rel_bias):
    B, S, H, Dh = q.shape
    nc = S // CHUNK
    pad = LEFT_CHUNKS * CHUNK
    kp = jnp.pad(k, ((0, 0), (pad, 0), (0, 0), (0, 0)))
    vp = jnp.pad(v, ((0, 0), (pad, 0), (0, 0), (0, 0)))
    rel = (LEFT_CHUNKS * CHUNK + np.arange(CHUNK)[:, None]) - np.arange(BAND)[None, :]
    idx = np.clip(rel, -MAX_REL, MAX_REL) + MAX_REL
    bias = rel_bias[:, idx].astype(jnp.float32)
    qc = q.reshape(B, nc, CHUNK, H, Dh).transpose(1, 0, 2, 3, 4)
    scale = HEAD_DIM ** -0.5
    key_off = jnp.arange(BAND, dtype=jnp.int32)

    def one_chunk(args):
        c, qb = args
        kb = lax.dynamic_slice_in_dim(kp, c * CHUNK, BAND, axis=1)
        vb = lax.dynamic_slice_in_dim(vp, c * CHUNK, BAND, axis=1)
        s = jnp.einsum('bqhd,bkhd->bhqk', qb, kb).astype(jnp.float32) * scale + bias
        valid = (c - LEFT_CHUNKS) * CHUNK + key_off >= 0
        s = jnp.where(valid, s, -1e30)
        p = jax.nn.softmax(s, axis=-1).astype(vb.dtype)
        return jnp.einsum('bhqk,bkhd->bqhd', p, vb)

    o = lax.map(one_chunk, (jnp.arange(nc, dtype=jnp.int32), qc))
    return o.transpose(1, 0, 2, 3, 4).reshape(B, S, H * Dh)


def causal_conv(u, w, b):
    S = u.shape[1]
    up = jnp.pad(u, ((0, 0), (CONV_W - 1, 0), (0, 0)))
    out = up[:, 0:S] * w[0]
    for j in range(1, CONV_W):
        out = out + up[:, j:j + S] * w[j]
    return out + b


def _lin_combine(left, right):
    a_l, b_l = left
    a_r, b_r = right
    return a_r * a_l, a_r * b_l + b_r


def rg_lru(u, w_rg, b_rg, w_ig, b_ig, L):
    B, S, _ = u.shape
    ub = u.reshape(B, S, LRU_BLOCKS, LRU_BLOCK)
    r = jax.nn.sigmoid(jnp.einsum('bsnc,ncd->bsnd', ub, w_rg).reshape(B, S, D_LRU) + b_rg)
    i = jax.nn.sigmoid(jnp.einsum('bsnc,ncd->bsnd', ub, w_ig).reshape(B, S, D_LRU) + b_ig)
    log_a = -LRU_C * r.astype(jnp.float32) * jax.nn.softplus(-L.astype(jnp.float32))
    a = jnp.exp(log_a)
    mult = jnp.sqrt(jnp.maximum(-jnp.expm1(2.0 * log_a), 0.0))
    b = mult * (i * u).astype(jnp.float32)
    _, h = lax.associative_scan(_lin_combine, (a, b), axis=1)
    return h.astype(u.dtype)


def setup_inputs(seed: int = 0) -> dict:
    key = jax.random.key(seed)
    ks = jax.random.split(key, 32)
    f32 = jnp.float32

    def w(k, shape, fan_in):
        return jax.random.normal(k, shape, f32) * (fan_in ** -0.5)

    def gain(k, shape):
        return 1.0 + 0.05 * jax.random.normal(k, shape, f32)

    def small(k, shape, s=0.01):
        return s * jax.random.normal(k, shape, f32)

    a0 = jax.random.uniform(ks[11], (DEPTH, D_LRU), f32, 0.9, 0.999) ** (1.0 / LRU_C)
    lru_L = jnp.log(a0) - jnp.log1p(-a0)
    return {
        "x": jax.random.normal(ks[0], (BATCH, SEQ, D_MODEL), f32),
        "mem": jax.random.normal(ks[1], (BATCH, N_MEM, D_MODEL), f32),
        "g_mix": gain(ks[2], (DEPTH, D_MODEL)),
        "w_in": w(ks[3], (DEPTH, D_MODEL, D_IN), D_MODEL),
        "rel_bias": 0.1 * jax.random.normal(ks[4], (DEPTH, ATT_HEADS, 2 * MAX_REL + 1), f32),
        "conv_w": w(ks[5], (DEPTH, CONV_W, D_LRU), CONV_W),
        "conv_b": small(ks[6], (DEPTH, D_LRU)),
        "w_rg": w(ks[7], (DEPTH, LRU_BLOCKS, LRU_BLOCK, LRU_BLOCK), LRU_BLOCK),
        "b_rg": small(ks[8], (DEPTH, D_LRU)),
        "w_ig": w(ks[9], (DEPTH, LRU_BLOCKS, LRU_BLOCK, LRU_BLOCK), LRU_BLOCK),
        "b_ig": small(ks[10], (DEPTH, D_LRU)),
        "lru_L": lru_L,
        "g_out_attn": gain(ks[12], (DEPTH, D_ATT)),
        "g_out_lru": gain(ks[13], (DEPTH, D_LRU)),
        "w_out": w(ks[14], (DEPTH, D_ATT + D_LRU, D_MODEL), D_ATT + D_LRU),
        "g_cross": gain(ks[15], (DEPTH, D_MODEL)),
        "g_mem": gain(ks[16], (DEPTH, D_MODEL)),
        "wq_c": w(ks[17], (DEPTH, D_MODEL, D_MODEL), D_MODEL),
        "wk_c": w(ks[18], (DEPTH, D_MODEL, D_MODEL), D_MODEL),
        "wv_c": w(ks[19], (DEPTH, D_MODEL, D_MODEL), D_MODEL),
        "wo_c": w(ks[20], (DEPTH, D_MODEL, D_MODEL), D_MODEL),
        "g_ffn": gain(ks[21], (DEPTH, D_MODEL)),
        "w_gate": w(ks[22], (DEPTH, D_MODEL, D_FF), D_MODEL),
        "w_up": w(ks[23], (DEPTH, D_MODEL, D_FF), D_MODEL),
        "w_down": w(ks[24], (DEPTH, D_FF, D_MODEL), D_FF),
        "g_final": gain(ks[25], (D_MODEL,)),
    }


def reference(x, mem, g_mix, w_in, rel_bias, conv_w, conv_b, w_rg, b_rg, w_ig, b_ig, lru_L,
              g_out_attn, g_out_lru, w_out, g_cross, g_mem, wq_c, wk_c, wv_c, wo_c,
              g_ffn, w_gate, w_up, w_down, g_final):
    B, S, _ = x.shape
    M = mem.shape[1]
    splits = [D_ATT, 2 * D_ATT, 3 * D_ATT, 3 * D_ATT + D_LRU]
    for l in range(DEPTH):
        h = rmsnorm(x, g_mix[l])
        proj = h @ w_in[l]
        q, k, v, xu, gu = jnp.split(proj, splits, axis=-1)
        att = chunk_attention(q.reshape(B, S, ATT_HEADS, HEAD_DIM),
                              k.reshape(B, S, ATT_HEADS, HEAD_DIM),
                              v.reshape(B, S, ATT_HEADS, HEAD_DIM), rel_bias[l])
        xu = causal_conv(xu, conv_w[l], conv_b[l])
        rec = rg_lru(xu, w_rg[l], b_rg[l], w_ig[l], b_ig[l], lru_L[l]) * jax.nn.gelu(gu)
        merged = jnp.concatenate([rmsnorm(att, g_out_attn[l]), rmsnorm(rec, g_out_lru[l])], axis=-1)
        x = x + merged @ w_out[l]

        hc = rmsnorm(x, g_cross[l])
        mn = rmsnorm(mem, g_mem[l])
        qx = (hc @ wq_c[l]).reshape(B, S, X_HEADS, X_HEAD_DIM)
        kx = (mn @ wk_c[l]).reshape(B, M, X_HEADS, X_HEAD_DIM)
        vx = (mn @ wv_c[l]).reshape(B, M, X_HEADS, X_HEAD_DIM)
        s = jnp.einsum('bshd,bmhd->bhsm', qx, kx).astype(jnp.float32) * (X_HEAD_DIM ** -0.5)
        p = jax.nn.softmax(s, axis=-1).astype(vx.dtype)
        ox = jnp.einsum('bhsm,bmhd->bshd', p, vx).reshape(B, S, D_MODEL)
        x = x + ox @ wo_c[l]

        hf = rmsnorm(x, g_ffn[l])
        x = x + (jax.nn.silu(hf @ w_gate[l]) * (hf @ w_up[l])) @ w_down[l]
    return rmsnorm(x, g_final)
```

```python
import functools

import numpy as np
import jax
import jax.numpy as jnp
from jax import lax
from jax.experimental import pallas as pl
from jax.experimental.pallas import tpu as pltpu

D_MODEL = 1024
CHUNK = 64
N_MEM = 256
ATT_HEADS = 8
HEAD_DIM = 64
D_ATT = ATT_HEADS * HEAD_DIM
D_LRU = D_MODEL - D_ATT
LRU_BLOCKS = 8
LRU_BLOCK = D_LRU // LRU_BLOCKS
CONV_W = 4
LRU_C = 8.0
LEFT_CHUNKS = 8
MAX_REL = 128
X_HEADS = 4
X_HEAD_DIM = D_MODEL // X_HEADS
D_FF = 2816
D_IN = 3 * D_ATT + 2 * D_LRU
EPS = 1e-6
NEG = -1e30

LANES = 128
SUBLANES = 8
LEFT = LEFT_CHUNKS * CHUNK
QB = 4 * CHUNK
KWIN = QB + LEFT
CWIN = 10 * CHUNK
HEADS_PER_GROUP = LANES // HEAD_DIM
N_GROUPS = D_ATT // LANES
TS = 512
TM = 512
FF_SPLITS = ((0, 1024), (1024, 2048), (2048, D_FF))
VMEM_LIMIT = 56 * 1024 * 1024

BF16 = jnp.bfloat16
F32 = jnp.float32


def _rmsnorm(x, g):
    return x * lax.rsqrt(jnp.mean(x * x, axis=-1, keepdims=True) + EPS) * g


def _dot(a, b):
    return jnp.dot(a, b, preferred_element_type=F32)


def _dot_t(a, b):
    return lax.dot_general(a, b, (((1,), (1,)), ((), ())), preferred_element_type=F32)


def _mem_kernel(mem_ref, g_ref, wk_ref, wv_ref, kx_ref, vx_ref):
    mn = _rmsnorm(mem_ref[...], g_ref[...]).astype(BF16)
    kx_ref[...] = (_dot(mn, wk_ref[...]) * (X_HEAD_DIM ** -0.5)).astype(BF16)
    vx_ref[...] = _dot(mn, wv_ref[...]).astype(BF16)


def _mem_proj(mem, g_mem, wk, wv):
    B, M, D = mem.shape
    const = lambda b: (0, 0)
    return pl.pallas_call(
        _mem_kernel,
        grid=(B,),
        in_specs=[
            pl.BlockSpec((None, M, D), lambda b: (b, 0, 0)),
            pl.BlockSpec((1, D), const),
            pl.BlockSpec((D, D), const),
            pl.BlockSpec((D, D), const),
        ],
        out_specs=[
            pl.BlockSpec((None, M, D), lambda b: (b, 0, 0)),
            pl.BlockSpec((None, M, D), lambda b: (b, 0, 0)),
        ],
        out_shape=[jax.ShapeDtypeStruct((B, M, D), BF16)] * 2,
        compiler_params=pltpu.CompilerParams(
            dimension_semantics=("arbitrary",), vmem_limit_bytes=VMEM_LIMIT),
        name="mem_proj",
    )(mem, g_mem, wk, wv)


def _attention_block(q, kbuf, vbuf, bias_ref, row0, key_pos0):
    lane = lax.broadcasted_iota(jnp.int32, (1, LANES), 1)
    col = lax.broadcasted_iota(jnp.int32, (1, CWIN), 1)
    outs = []
    for grp in range(N_GROUPS):
        cols = slice(grp * LANES, (grp + 1) * LANES)
        qg = q[:, cols]
        kw = kbuf[pl.ds(row0, KWIN), cols]
        vw = vbuf[pl.ds(row0, KWIN), cols]
        o_grp = None
        for e in range(HEADS_PER_GROUP):
            head = grp * HEADS_PER_GROUP + e
            in_head = (lane // HEAD_DIM) == e
            s = _dot_t(jnp.where(in_head, qg, jnp.zeros_like(qg)), kw)
            p_rows, l_rows = [], []
            for i in range(QB // CHUNK):
                w0 = (i // 2) * LANES
                started = (key_pos0 + w0 + col) >= 0
                si = (s[i * CHUNK:(i + 1) * CHUNK, w0:w0 + CWIN]
                      + bias_ref[head, i % 2]
                      + jnp.where(started, 0.0, NEG))
                m = jnp.max(si, axis=-1, keepdims=True)
                p = jnp.exp(si - m)
                l_rows.append(jnp.sum(p, axis=-1, keepdims=True))
                p = p.astype(BF16)
                pad = jnp.zeros((CHUNK, KWIN - CWIN), BF16)
                p_rows.append(jnp.concatenate([p, pad] if w0 == 0 else [pad, p], axis=1))
            p_all = jnp.concatenate(p_rows, axis=0)
            l_all = jnp.concatenate(l_rows, axis=0)
            o = _dot(p_all, vw) / l_all
            o_grp = o if e == 0 else jnp.where(in_head, o, o_grp)
        outs.append(o_grp)
    return jnp.concatenate(outs, axis=1)


def _mixer_kernel(x_ref, g_mix_ref, w_in_ref, bias_ref, conv_w_ref, conv_b_ref,
                  w_rg_ref, b_rg_ref, w_ig_ref, b_ig_ref, lru_l_ref,
                  g_att_ref, g_lru_ref, w_out_ref,
                  o_ref,
                  kbuf, vbuf, ubuf, hbuf, hcar):
    t = pl.program_id(1)

    @pl.when(t == 0)
    def _():
        kbuf[0:LEFT, :] = jnp.zeros((LEFT, D_ATT), BF16)
        vbuf[0:LEFT, :] = jnp.zeros((LEFT, D_ATT), BF16)
        ubuf[0:SUBLANES, :] = jnp.zeros((SUBLANES, D_LRU), F32)
        hcar[...] = jnp.zeros_like(hcar)

    x = x_ref[...]
    h = _rmsnorm(x, g_mix_ref[...]).astype(BF16)

    q = (_dot(h, w_in_ref[:, 0:D_ATT]) * (HEAD_DIM ** -0.5)).astype(BF16)
    kbuf[LEFT:LEFT + TS, :] = _dot(h, w_in_ref[:, D_ATT:2 * D_ATT]).astype(BF16)
    vbuf[LEFT:LEFT + TS, :] = _dot(h, w_in_ref[:, 2 * D_ATT:3 * D_ATT]).astype(BF16)
    ubuf[SUBLANES:SUBLANES + TS, :] = _dot(h, w_in_ref[:, 3 * D_ATT:3 * D_ATT + D_LRU])
    gu = _dot(h, w_in_ref[:, 3 * D_ATT + D_LRU:D_IN])

    att_blocks = []
    for qb in range(TS // QB):
        att_blocks.append(_attention_block(
            q[qb * QB:(qb + 1) * QB], kbuf, vbuf, bias_ref,
            qb * QB, t * TS + qb * QB - LEFT))
    att = jnp.concatenate(att_blocks, axis=0)
    kbuf[0:LEFT, :] = kbuf[TS:TS + LEFT, :]
    vbuf[0:LEFT, :] = vbuf[TS:TS + LEFT, :]

    xc = conv_b_ref[...]
    for j in range(CONV_W):
        off = SUBLANES - (CONV_W - 1) + j
        xc = xc + ubuf[off:off + TS, :] * conv_w_ref[j:j + 1, :]
    ubuf[0:SUBLANES, :] = ubuf[TS:TS + SUBLANES, :]

    xcb = xc.astype(BF16)
    half = D_LRU // 2
    pre_r = jnp.concatenate([_dot(xcb[:, :half], w_rg_ref[0]), _dot(xcb[:, half:], w_rg_ref[1])], axis=1)
    pre_i = jnp.concatenate([_dot(xcb[:, :half], w_ig_ref[0]), _dot(xcb[:, half:], w_ig_ref[1])], axis=1)
    r = jax.nn.sigmoid(pre_r + b_rg_ref[...])
    ig = jax.nn.sigmoid(pre_i + b_ig_ref[...])
    neg_l = -lru_l_ref[...]
    softplus = jnp.maximum(neg_l, 0.0) + jnp.log1p(jnp.exp(-jnp.abs(neg_l)))
    log_a = (-LRU_C * softplus) * r
    a = jnp.exp(log_a)
    mult = jnp.sqrt(jnp.maximum(-jnp.tanh(log_a) * (a * a + 1.0), 0.0))
    b = mult * (ig * xc)

    row = lax.broadcasted_iota(jnp.int32, (TS, D_LRU), 0) % SUBLANES
    for d in (1, 2, 4):
        a_prev = pltpu.roll(a, d, axis=0)
        b_prev = pltpu.roll(b, d, axis=0)
        keep = row >= d
        b = jnp.where(keep, a * b_prev + b, b)
        a = jnp.where(keep, a * a_prev, a)
    carry = hcar[0:1, :]
    for g in range(TS // SUBLANES):
        rows = slice(g * SUBLANES, (g + 1) * SUBLANES)
        hg = a[rows] * carry + b[rows]
        hbuf[rows, :] = hg
        carry = hg[SUBLANES - 1:SUBLANES, :]
    hcar[0:1, :] = carry
    rec = hbuf[...] * jax.nn.gelu(gu)

    merged = jnp.concatenate(
        [_rmsnorm(att, g_att_ref[...]), _rmsnorm(rec, g_lru_ref[...])], axis=1).astype(BF16)
    o_ref[...] = x + _dot(merged, w_out_ref[...])


def _const_spec(shape):
    nd = len(shape)
    return pl.BlockSpec(shape, lambda b, t: (0,) * nd, pipeline_mode=pl.Buffered(1))


def _mixer(x, g_mix, w_in, bias_tbl, conv_w, conv_b, w_rg, b_rg, w_ig, b_ig, lru_l,
           g_att, g_lru, w_out):
    B, S, D = x.shape
    tile = pl.BlockSpec((None, TS, D), lambda b, t: (b, t, 0))
    args = (g_mix, w_in, bias_tbl, conv_w, conv_b, w_rg, b_rg, w_ig, b_ig, lru_l,
            g_att, g_lru, w_out)
    return pl.pallas_call(
        _mixer_kernel,
        grid=(B, S // TS),
        in_specs=[tile] + [_const_spec(a.shape) for a in args],
        out_specs=tile,
        out_shape=jax.ShapeDtypeStruct((B, S, D), F32),
        scratch_shapes=[
            pltpu.VMEM((LEFT + TS, D_ATT), BF16),
            pltpu.VMEM((LEFT + TS, D_ATT), BF16),
            pltpu.VMEM((SUBLANES + TS, D_LRU), F32),
            pltpu.VMEM((TS, D_LRU), F32),
            pltpu.VMEM((SUBLANES, D_LRU), F32),
        ],
        compiler_params=pltpu.CompilerParams(
            dimension_semantics=("arbitrary", "arbitrary"), vmem_limit_bytes=VMEM_LIMIT),
        name="mixer",
    )(x, *args)


def _cross_ffn_kernel(x_ref, kx_ref, vx_ref, g_cross_ref, wq_ref, wo_ref,
                      g_ffn_ref, w_gate_ref, w_up_ref, w_down_ref, g_final_ref, o_ref):
    x = x_ref[...]
    hc = _rmsnorm(x, g_cross_ref[...]).astype(BF16)
    q = _dot(hc, wq_ref[...]).astype(BF16)
    heads = []
    for hd in range(X_HEADS):
        cols = slice(hd * X_HEAD_DIM, (hd + 1) * X_HEAD_DIM)
        s = _dot_t(q[:, cols], kx_ref[:, cols])
        m = jnp.max(s, axis=-1, keepdims=True)
        p = jnp.exp(s - m)
        l = jnp.sum(p, axis=-1, keepdims=True)
        heads.append((_dot(p.astype(BF16), vx_ref[:, cols]) / l).astype(BF16))
    x = x + _dot(jnp.concatenate(heads, axis=1), wo_ref[...])

    hf = _rmsnorm(x, g_ffn_ref[...]).astype(BF16)
    y = x
    for lo, hi in FF_SPLITS:
        gate = _dot(hf, w_gate_ref[:, lo:hi])
        up = _dot(hf, w_up_ref[:, lo:hi])
        y = y + _dot((jax.nn.silu(gate) * up).astype(BF16), w_down_ref[lo:hi, :])
    o_ref[...] = _rmsnorm(y, g_final_ref[...])


def _cross_ffn(x, kx, vx, g_cross, wq, wo, g_ffn, w_gate, w_up, w_down, g_final):
    B, S, D = x.shape
    M = kx.shape[1]
    tile = pl.BlockSpec((None, TM, D), lambda b, t: (b, t, 0))
    memspec = pl.BlockSpec((None, M, D), lambda b, t: (b, 0, 0))
    args = (g_cross, wq, wo, g_ffn, w_gate, w_up, w_down, g_final)
    return pl.pallas_call(
        _cross_ffn_kernel,
        grid=(B, S // TM),
        in_specs=[tile, memspec, memspec] + [_const_spec(a.shape) for a in args],
        out_specs=tile,
        out_shape=jax.ShapeDtypeStruct((B, S, D), F32),
        compiler_params=pltpu.CompilerParams(
            dimension_semantics=("arbitrary", "arbitrary"), vmem_limit_bytes=VMEM_LIMIT),
        name="cross_ffn",
    )(x, kx, vx, *args)


def _bias_table(rel_bias):
    tbl = []
    qi = np.arange(CHUNK)[:, None]
    kj = np.arange(CWIN)[None, :]
    for parity in range(2):
        rel = (parity + LEFT_CHUNKS) * CHUNK + qi - kj
        slot = kj // CHUNK
        band = np.broadcast_to((slot >= parity) & (slot <= parity + LEFT_CHUNKS), rel.shape)
        idx = np.clip(rel, -MAX_REL, MAX_REL) + MAX_REL
        tbl.append(jnp.where(band[None], rel_bias[:, idx].astype(F32), NEG))
    return jnp.stack(tbl, axis=1)


def _block_diag(w):
    per = LRU_BLOCKS // 2
    halves = []
    for hlf in range(2):
        rows = []
        for i in range(per):
            blocks = [w[hlf * per + i] if j == i else jnp.zeros((LRU_BLOCK, LRU_BLOCK), w.dtype)
                      for j in range(per)]
            rows.append(jnp.concatenate(blocks, axis=1))
        halves.append(jnp.concatenate(rows, axis=0))
    return jnp.stack(halves).astype(BF16)


def kernel(x, mem, g_mix, w_in, rel_bias, conv_w, conv_b, w_rg, b_rg, w_ig, b_ig, lru_L,
           g_out_attn, g_out_lru, w_out, g_cross, g_mem, wq_c, wk_c, wv_c, wo_c,
           g_ffn, w_gate, w_up, w_down, g_final):
    depth = g_mix.shape[0]
    row = lambda v: v.reshape(1, -1)
    bf = lambda w: w.astype(BF16)
    for l in range(depth):
        kx, vx = _mem_proj(mem, row(g_mem[l]), bf(wk_c[l]), bf(wv_c[l]))
        x = _mixer(x, row(g_mix[l]), bf(w_in[l]), _bias_table(rel_bias[l]),
                   conv_w[l], row(conv_b[l]), _block_diag(w_rg[l]), row(b_rg[l]),
                   _block_diag(w_ig[l]), row(b_ig[l]), row(lru_L[l]),
                   row(g_out_attn[l]), row(g_out_lru[l]), bf(w_out[l]))
        assert depth == 1
        x = _cross_ffn(x, kx, vx, row(g_cross[l]), bf(wq_c[l]), bf(wo_c[l]), row(g_ffn[l]),
                       bf(w_gate[l]), bf(w_up[l]), bf(w_down[l]), row(g_final))
    return x
```

```python
import functools

import numpy as np
import jax
import jax.numpy as jnp
from jax import lax
from jax.experimental import pallas as pl
from jax.experimental.pallas import tpu as pltpu

D_MODEL = 1024
CHUNK = 64
N_MEM = 256
ATT_HEADS = 8
HEAD_DIM = 64
D_ATT = ATT_HEADS * HEAD_DIM
D_LRU = D_MODEL - D_ATT
LRU_BLOCKS = 8
LRU_BLOCK = D_LRU // LRU_BLOCKS
CONV_W = 4
LRU_C = 8.0
LEFT_CHUNKS = 8
MAX_REL = 128
X_HEADS = 4
X_HEAD_DIM = D_MODEL // X_HEADS
D_FF = 2816
D_IN = 3 * D_ATT + 2 * D_LRU
EPS = 1e-6
NEG = -1e30
LOG2E = 1.4426950408889634

LANES = 128
SUBLANES = 8
LEFT = LEFT_CHUNKS * CHUNK
QB = 4 * CHUNK
KWIN = QB + LEFT
CWIN = 10 * CHUNK
HEADS_PER_GROUP = LANES // HEAD_DIM
N_GROUPS = D_ATT // LANES
TS = 512
TM = 512
FF_SPLITS = ((0, 1024), (1024, 2048), (2048, D_FF))
VMEM_LIMIT = 56 * 1024 * 1024

BF16 = jnp.bfloat16
F32 = jnp.float32


def _rmsnorm(x, g):
    return x * lax.rsqrt(jnp.mean(x * x, axis=-1, keepdims=True) + EPS) * g


def _dot(a, b):
    return jnp.dot(a, b, preferred_element_type=F32)


def _dot_t(a, b):
    return lax.dot_general(a, b, (((1,), (1,)), ((), ())), preferred_element_type=F32)


def _mem_kernel(mem_ref, g_ref, wk_ref, wv_ref, kx_ref, vx_ref):
    mn = _rmsnorm(mem_ref[...], g_ref[...]).astype(BF16)
    kx_ref[...] = (_dot(mn, wk_ref[...]) * (X_HEAD_DIM ** -0.5)).astype(BF16)
    vx_ref[...] = _dot(mn, wv_ref[...]).astype(BF16)


def _mem_proj(mem, g_mem, wk, wv):
    B, M, D = mem.shape
    const = lambda b: (0, 0)
    return pl.pallas_call(
        _mem_kernel,
        grid=(B,),
        in_specs=[
            pl.BlockSpec((None, M, D), lambda b: (b, 0, 0)),
            pl.BlockSpec((1, D), const),
            pl.BlockSpec((D, D), const),
            pl.BlockSpec((D, D), const),
        ],
        out_specs=[
            pl.BlockSpec((None, M, D), lambda b: (b, 0, 0)),
            pl.BlockSpec((None, M, D), lambda b: (b, 0, 0)),
        ],
        out_shape=[jax.ShapeDtypeStruct((B, M, D), BF16)] * 2,
        compiler_params=pltpu.CompilerParams(
            dimension_semantics=("arbitrary",), vmem_limit_bytes=VMEM_LIMIT),
        name="mem_proj",
    )(mem, g_mem, wk, wv)


def _head_mask(e):
    lane = lax.broadcasted_iota(jnp.int32, (1, LANES), 1)
    return (lane // HEAD_DIM) == e


def _scores(q, kbuf, qb, grp, e):
    cols = slice(grp * LANES, (grp + 1) * LANES)
    qg = q[qb * QB:(qb + 1) * QB, cols]
    return _dot_t(jnp.where(_head_mask(e), qg, jnp.zeros_like(qg)),
                  kbuf[qb * QB:qb * QB + KWIN, cols])


def _probs(s, bias_ref, head, key_pos0):
    col = lax.broadcasted_iota(jnp.int32, (1, CWIN), 1)
    p_rows = []
    for i in range(QB // CHUNK):
        w0 = (i // 2) * LANES
        si = s[i * CHUNK:(i + 1) * CHUNK, w0:w0 + CWIN] + bias_ref[head, i % 2]
        if key_pos0 is not None:
            si = si + jnp.where((key_pos0 + w0 + col) >= 0, 0.0, NEG)
        p = jnp.exp2(si - jnp.max(si, axis=-1, keepdims=True)).astype(BF16)
        pad = jnp.zeros((CHUNK, KWIN - CWIN), BF16)
        p_rows.append(jnp.concatenate([p, pad] if w0 == 0 else [pad, p], axis=1))
    return jnp.concatenate(p_rows, axis=0)


def _weighted_values(p, vbuf, qb, grp, e):
    vw = vbuf[qb * QB:qb * QB + KWIN, grp * LANES:(grp + 1) * LANES]
    o = _dot(p, jnp.where(_head_mask(e), vw, jnp.ones_like(vw)))
    return o / pltpu.roll(o, HEAD_DIM, axis=1)


def _attention(q, kbuf, vbuf, bias_ref, abuf, first_tile):
    order = [(qb, grp, e) for qb in range(TS // QB) for grp in range(N_GROUPS)
             for e in range(HEADS_PER_GROUP)]
    s_next = _scores(q, kbuf, *order[0])
    o_first = None
    for j, (qb, grp, e) in enumerate(order):
        s_cur = s_next
        if j + 1 < len(order):
            s_next = _scores(q, kbuf, *order[j + 1])
        p = _probs(s_cur, bias_ref, grp * HEADS_PER_GROUP + e,
                   qb * QB - LEFT if first_tile else None)
        o = _weighted_values(p, vbuf, qb, grp, e)
        if e == 0:
            o_first = o
        else:
            abuf[qb * QB:(qb + 1) * QB, grp * LANES:(grp + 1) * LANES] = (
                jnp.where(_head_mask(e), o, o_first))


def _mixer_kernel(x_ref, g_mix_ref, w_in_ref, bias_ref, conv_w_ref, conv_b_ref,
                  w_rg_ref, b_rg_ref, w_ig_ref, b_ig_ref, lru_l_ref,
                  g_att_ref, g_lru_ref, w_out_ref,
                  o_ref,
                  kbuf, vbuf, ubuf, abuf, hbuf, hcar):
    t = pl.program_id(1)

    @pl.when(t == 0)
    def _():
        kbuf[0:LEFT, :] = jnp.zeros((LEFT, D_ATT), BF16)
        vbuf[0:LEFT, :] = jnp.zeros((LEFT, D_ATT), BF16)
        ubuf[0:SUBLANES, :] = jnp.zeros((SUBLANES, D_LRU), F32)
        hcar[...] = jnp.zeros_like(hcar)

    x = x_ref[...]
    h = _rmsnorm(x, g_mix_ref[...]).astype(BF16)

    q = (_dot(h, w_in_ref[:, 0:D_ATT]) * (HEAD_DIM ** -0.5 * LOG2E)).astype(BF16)
    kbuf[LEFT:LEFT + TS, :] = _dot(h, w_in_ref[:, D_ATT:2 * D_ATT]).astype(BF16)
    vbuf[LEFT:LEFT + TS, :] = _dot(h, w_in_ref[:, 2 * D_ATT:3 * D_ATT]).astype(BF16)
    ubuf[SUBLANES:SUBLANES + TS, :] = _dot(h, w_in_ref[:, 3 * D_ATT:3 * D_ATT + D_LRU])
    gu = _dot(h, w_in_ref[:, 3 * D_ATT + D_LRU:D_IN])

    pl.when(t == 0)(functools.partial(_attention, q, kbuf, vbuf, bias_ref, abuf, True))
    pl.when(t > 0)(functools.partial(_attention, q, kbuf, vbuf, bias_ref, abuf, False))
    att = abuf[...]
    kbuf[0:LEFT, :] = kbuf[TS:TS + LEFT, :]
    vbuf[0:LEFT, :] = vbuf[TS:TS + LEFT, :]

    xc = conv_b_ref[...]
    for j in range(CONV_W):
        off = SUBLANES - (CONV_W - 1) + j
        xc = xc + ubuf[off:off + TS, :] * conv_w_ref[j:j + 1, :]
    ubuf[0:SUBLANES, :] = ubuf[TS:TS + SUBLANES, :]

    xcb = xc.astype(BF16)
    half = D_LRU // 2
    pre_r = jnp.concatenate([_dot(xcb[:, :half], w_rg_ref[0]), _dot(xcb[:, half:], w_rg_ref[1])], axis=1)
    pre_i = jnp.concatenate([_dot(xcb[:, :half], w_ig_ref[0]), _dot(xcb[:, half:], w_ig_ref[1])], axis=1)
    r = jax.nn.sigmoid(pre_r + b_rg_ref[...])
    ig = jax.nn.sigmoid(pre_i + b_ig_ref[...])
    neg_l = -lru_l_ref[...]
    softplus = jnp.maximum(neg_l, 0.0) + jnp.log1p(jnp.exp(-jnp.abs(neg_l)))
    log_a = (-LRU_C * softplus) * r
    a = jnp.exp(log_a)
    mult = jnp.sqrt(jnp.maximum(-jnp.tanh(log_a) * (a * a + 1.0), 0.0))
    b = mult * (ig * xc)

    groups = TS // SUBLANES
    a = a.reshape(groups, SUBLANES, D_LRU)
    b = b.reshape(groups, SUBLANES, D_LRU)
    row = lax.broadcasted_iota(jnp.int32, (1, SUBLANES, D_LRU), 1)
    for d in (1, 2, 4):
        keep = row >= d
        b = jnp.where(keep, a * pltpu.roll(b, d, axis=1) + b, b)
        a = jnp.where(keep, a * pltpu.roll(a, d, axis=1), a)
    carry = hcar[0:1, :]
    for g in range(groups):
        hg = a[g] * carry + b[g]
        hbuf[g * SUBLANES:(g + 1) * SUBLANES, :] = hg
        carry = hg[SUBLANES - 1:SUBLANES, :]
    hcar[0:1, :] = carry
    rec = hbuf[...] * jax.nn.gelu(gu)

    merged = jnp.concatenate(
        [_rmsnorm(att, g_att_ref[...]), _rmsnorm(rec, g_lru_ref[...])], axis=1).astype(BF16)
    o_ref[...] = x + _dot(merged, w_out_ref[...])


def _const_spec(shape):
    nd = len(shape)
    return pl.BlockSpec(shape, lambda b, t: (0,) * nd, pipeline_mode=pl.Buffered(1))


def _mixer(x, g_mix, w_in, bias_tbl, conv_w, conv_b, w_rg, b_rg, w_ig, b_ig, lru_l,
           g_att, g_lru, w_out):
    B, S, D = x.shape
    tile = pl.BlockSpec((None, TS, D), lambda b, t: (b, t, 0))
    args = (g_mix, w_in, bias_tbl, conv_w, conv_b, w_rg, b_rg, w_ig, b_ig, lru_l,
            g_att, g_lru, w_out)
    return pl.pallas_call(
        _mixer_kernel,
        grid=(B, S // TS),
        in_specs=[tile] + [_const_spec(a.shape) for a in args],
        out_specs=tile,
        out_shape=jax.ShapeDtypeStruct((B, S, D), F32),
        scratch_shapes=[
            pltpu.VMEM((LEFT + TS, D_ATT), BF16),
            pltpu.VMEM((LEFT + TS, D_ATT), BF16),
            pltpu.VMEM((SUBLANES + TS, D_LRU), F32),
            pltpu.VMEM((TS, D_ATT), F32),
            pltpu.VMEM((TS, D_LRU), F32),
            pltpu.VMEM((SUBLANES, D_LRU), F32),
        ],
        compiler_params=pltpu.CompilerParams(
            dimension_semantics=("arbitrary", "arbitrary"), vmem_limit_bytes=VMEM_LIMIT),
        name="mixer",
    )(x, *args)


def _cross_ffn_kernel(x_ref, kx_ref, vx_ref, g_cross_ref, wq_ref, wo_ref,
                      g_ffn_ref, w_gate_ref, w_up_ref, w_down_ref, g_final_ref, o_ref):
    x = x_ref[...]
    hc = _rmsnorm(x, g_cross_ref[...]).astype(BF16)
    q = _dot(hc, wq_ref[...]).astype(BF16)
    heads = []
    for hd in range(X_HEADS):
        cols = slice(hd * X_HEAD_DIM, (hd + 1) * X_HEAD_DIM)
        s = _dot_t(q[:, cols], kx_ref[:, cols])
        m = jnp.max(s, axis=-1, keepdims=True)
        p = jnp.exp(s - m)
        l = jnp.sum(p, axis=-1, keepdims=True)
        heads.append((_dot(p.astype(BF16), vx_ref[:, cols]) / l).astype(BF16))
    x = x + _dot(jnp.concatenate(heads, axis=1), wo_ref[...])

    hf = _rmsnorm(x, g_ffn_ref[...]).astype(BF16)
    y = x
    for lo, hi in FF_SPLITS:
        gate = _dot(hf, w_gate_ref[:, lo:hi])
        up = _dot(hf, w_up_ref[:, lo:hi])
        y = y + _dot((jax.nn.silu(gate) * up).astype(BF16), w_down_ref[lo:hi, :])
    o_ref[...] = _rmsnorm(y, g_final_ref[...])


def _cross_ffn(x, kx, vx, g_cross, wq, wo, g_ffn, w_gate, w_up, w_down, g_final):
    B, S, D = x.shape
    M = kx.shape[1]
    tile = pl.BlockSpec((None, TM, D), lambda b, t: (b, t, 0))
    memspec = pl.BlockSpec((None, M, D), lambda b, t: (b, 0, 0))
    args = (g_cross, wq, wo, g_ffn, w_gate, w_up, w_down, g_final)
    return pl.pallas_call(
        _cross_ffn_kernel,
        grid=(B, S // TM),
        in_specs=[tile, memspec, memspec] + [_const_spec(a.shape) for a in args],
        out_specs=tile,
        out_shape=jax.ShapeDtypeStruct((B, S, D), F32),
        compiler_params=pltpu.CompilerParams(
            dimension_semantics=("arbitrary", "arbitrary"), vmem_limit_bytes=VMEM_LIMIT),
        name="cross_ffn",
    )(x, kx, vx, *args)


def _bias_table(rel_bias):
    n_heads = rel_bias.shape[0]
    rb = rel_bias.astype(F32) * LOG2E
    slot = np.arange(CWIN) // CHUNK
    tbl = []
    for parity in range(2):
        rel_max = (parity + LEFT_CHUNKS) * CHUNK + (CHUNK - 1)
        rel_min = (parity + LEFT_CHUNKS) * CHUNK - (CWIN - 1)
        hi, lo = min(rel_max, MAX_REL), max(rel_min, -MAX_REL)
        vec = jnp.concatenate([
            jnp.broadcast_to(rb[:, 2 * MAX_REL:], (n_heads, rel_max - hi)),
            jnp.flip(rb[:, lo + MAX_REL:hi + MAX_REL + 1], axis=1),
            jnp.broadcast_to(rb[:, :1], (n_heads, lo - rel_min)),
        ], axis=1)
        rows = jnp.stack([vec[:, CHUNK - 1 - qi:CHUNK - 1 - qi + CWIN] for qi in range(CHUNK)],
                         axis=1)
        band = (slot >= parity) & (slot <= parity + LEFT_CHUNKS)
        tbl.append(jnp.where(band[None, None, :], rows, NEG))
    return jnp.stack(tbl, axis=1)


def _block_diag(w):
    per = LRU_BLOCKS // 2
    halves = []
    for hlf in range(2):
        rows = []
        for i in range(per):
            blocks = [w[hlf * per + i] if j == i else jnp.zeros((LRU_BLOCK, LRU_BLOCK), w.dtype)
                      for j in range(per)]
            rows.append(jnp.concatenate(blocks, axis=1))
        halves.append(jnp.concatenate(rows, axis=0))
    return jnp.stack(halves).astype(BF16)


def kernel(x, mem, g_mix, w_in, rel_bias, conv_w, conv_b, w_rg, b_rg, w_ig, b_ig, lru_L,
           g_out_attn, g_out_lru, w_out, g_cross, g_mem, wq_c, wk_c, wv_c, wo_c,
           g_ffn, w_gate, w_up, w_down, g_final):
    depth = g_mix.shape[0]
    row = lambda v: v.reshape(1, -1)
    bf = lambda w: w.astype(BF16)
    for l in range(depth):
        kx, vx = _mem_proj(mem, row(g_mem[l]), bf(wk_c[l]), bf(wv_c[l]))
        x = _mixer(x, row(g_mix[l]), bf(w_in[l]), _bias_table(rel_bias[l]),
                   conv_w[l], row(conv_b[l]), _block_diag(w_rg[l]), row(b_rg[l]),
                   _block_diag(w_ig[l]), row(b_ig[l]), row(lru_L[l]),
                   row(g_out_attn[l]), row(g_out_lru[l]), bf(w_out[l]))
        assert depth == 1
        x = _cross_ffn(x, kx, vx, row(g_cross[l]), bf(wq_c[l]), bf(wo_c[l]), row(g_ffn[l]),
                       bf(w_gate[l]), bf(w_up[l]), bf(w_down[l]), row(g_final))
    return x
```

```python
import functools

import numpy as np
import jax
import jax.numpy as jnp
from jax import lax
from jax.experimental import pallas as pl
from jax.experimental.pallas import tpu as pltpu

D_MODEL = 1024
CHUNK = 64
N_MEM = 256
ATT_HEADS = 8
HEAD_DIM = 64
D_ATT = ATT_HEADS * HEAD_DIM
D_LRU = D_MODEL - D_ATT
LRU_BLOCKS = 8
LRU_BLOCK = D_LRU // LRU_BLOCKS
CONV_W = 4
LRU_C = 8.0
LEFT_CHUNKS = 8
MAX_REL = 128
X_HEADS = 4
X_HEAD_DIM = D_MODEL // X_HEADS
D_FF = 2816
D_IN = 3 * D_ATT + 2 * D_LRU
EPS = 1e-6
NEG = -1e30
LOG2E = 1.4426950408889634

LANES = 128
SUBLANES = 8
LEFT = LEFT_CHUNKS * CHUNK
QB = 4 * CHUNK
KWIN = QB + LEFT
CWIN = 10 * CHUNK
HEADS_PER_GROUP = LANES // HEAD_DIM
N_GROUPS = D_ATT // LANES
TS = 512
SLAB = 32
MXU_N = 256
TM = 512
FF_SPLITS = ((0, 1024), (1024, 2048), (2048, D_FF))
VMEM_LIMIT = 56 * 1024 * 1024

BF16 = jnp.bfloat16
F32 = jnp.float32


def _rmsnorm(x, g):
    return x * lax.rsqrt(jnp.mean(x * x, axis=-1, keepdims=True) + EPS) * g


def _dot(a, b):
    return jnp.dot(a, b, preferred_element_type=F32)


def _dot_t(a, b):
    return lax.dot_general(a, b, (((1,), (1,)), ((), ())), preferred_element_type=F32)


def _mem_kernel(mem_ref, g_ref, wk_ref, wv_ref, kx_ref, vx_ref):
    mn = _rmsnorm(mem_ref[...], g_ref[...]).astype(BF16)
    kx_ref[...] = (_dot(mn, wk_ref[...]) * (X_HEAD_DIM ** -0.5)).astype(BF16)
    vx_ref[...] = _dot(mn, wv_ref[...]).astype(BF16)


def _mem_proj(mem, g_mem, wk, wv):
    B, M, D = mem.shape
    const = lambda b: (0, 0)
    return pl.pallas_call(
        _mem_kernel,
        grid=(B,),
        in_specs=[
            pl.BlockSpec((None, M, D), lambda b: (b, 0, 0)),
            pl.BlockSpec((1, D), const),
            pl.BlockSpec((D, D), const),
            pl.BlockSpec((D, D), const),
        ],
        out_specs=[
            pl.BlockSpec((None, M, D), lambda b: (b, 0, 0)),
            pl.BlockSpec((None, M, D), lambda b: (b, 0, 0)),
        ],
        out_shape=[jax.ShapeDtypeStruct((B, M, D), BF16)] * 2,
        compiler_params=pltpu.CompilerParams(
            dimension_semantics=("arbitrary",), vmem_limit_bytes=VMEM_LIMIT),
        name="mem_proj",
    )(mem, g_mem, wk, wv)


def _head_mask(e):
    lane = lax.broadcasted_iota(jnp.int32, (1, LANES), 1)
    return (lane // HEAD_DIM) == e


def _head_lanes(e):
    lane = lax.broadcasted_iota(jnp.int32, (1, D_ATT), 1)
    return (lane % LANES) // HEAD_DIM == e


def _scores(q_s, kbufs, qb, grp, e):
    cols = slice(grp * LANES, (grp + 1) * LANES)
    lane = lax.broadcasted_iota(jnp.int32, (1, LANES), 1)
    flag = jnp.where(lane == (1 - e) * HEAD_DIM, 1.0, 0.0).astype(BF16)
    qg = q_s[qb * QB:(qb + 1) * QB, cols]
    return _dot_t(jnp.where(_head_mask(e), qg, flag), kbufs[e][qb * QB:qb * QB + KWIN, cols])


def _probs(s, bias_ref, head):
    p_rows = []
    for i in range(QB // CHUNK):
        w0 = (i // 2) * LANES
        si = s[i * CHUNK:(i + 1) * CHUNK, w0:w0 + CWIN] + bias_ref[head, i % 2]
        p = jnp.exp2(si - jnp.max(si, axis=-1, keepdims=True)).astype(BF16)
        pad = jnp.zeros((CHUNK, KWIN - CWIN), BF16)
        p_rows.append(jnp.concatenate([p, pad] if w0 == 0 else [pad, p], axis=1))
    return jnp.concatenate(p_rows, axis=0)


def _weighted_values(p, vbufs, qb, grp, e):
    o = _dot(p, vbufs[e][qb * QB:qb * QB + KWIN, grp * LANES:(grp + 1) * LANES])
    return o / pltpu.roll(o, HEAD_DIM, axis=1)


def _attention(q_s, kbufs, vbufs, bias_ref, abuf, between):
    order = [(qb, grp, e) for qb in range(TS // QB) for grp in range(N_GROUPS)
             for e in range(HEADS_PER_GROUP)]
    s_next = _scores(q_s, kbufs, *order[0])
    o_first = None
    for j, (qb, grp, e) in enumerate(order):
        s_cur = s_next
        if j + 1 < len(order):
            s_next = _scores(q_s, kbufs, *order[j + 1])
        p = _probs(s_cur, bias_ref, grp * HEADS_PER_GROUP + e)
        o = _weighted_values(p, vbufs, qb, grp, e)
        if e == 0:
            o_first = o
        else:
            abuf[qb * QB:(qb + 1) * QB, grp * LANES:(grp + 1) * LANES] = (
                jnp.where(_head_mask(e), o, o_first))
        between(j)


def _mixer_kernel(x_ref, g_mix_ref, w_in_ref, bias_ref, conv_w_ref, conv_b_ref,
                  w_rg_ref, b_rg_ref, w_ig_ref, b_ig_ref, lru_l_ref,
                  g_att_ref, g_lru_ref, w_out_ref,
                  o_ref,
                  kbuf0, kbuf1, vbuf0, vbuf1, q_s, ubuf, gbuf, xc_s, xcb_s, pr_s, pi_s,
                  a_s, b_s, hbuf, abuf, m_s, hcar):
    t = pl.program_id(1)
    kbufs, vbufs = (kbuf0, kbuf1), (vbuf0, vbuf1)

    @pl.when(t == 0)
    def _():
        for e in range(HEADS_PER_GROUP):
            own = jnp.broadcast_to(_head_lanes(e), (LEFT, D_ATT))
            kbufs[e][0:LEFT, :] = jnp.where(own, 0.0, NEG).astype(BF16)
            vbufs[e][0:LEFT, :] = jnp.where(own, 0.0, 1.0).astype(BF16)
        ubuf[0:SUBLANES, :] = jnp.zeros((SUBLANES, D_LRU), F32)
        hcar[...] = jnp.zeros_like(hcar)

    x = x_ref[...]
    h = _rmsnorm(x, g_mix_ref[...]).astype(BF16)
    q_s[...] = (_dot(h, w_in_ref[:, 0:D_ATT]) * (HEAD_DIM ** -0.5 * LOG2E)).astype(BF16)
    kf = _dot(h, w_in_ref[:, D_ATT:2 * D_ATT])
    vf = _dot(h, w_in_ref[:, 2 * D_ATT:3 * D_ATT])
    for e in range(HEADS_PER_GROUP):
        kbufs[e][LEFT:LEFT + TS, :] = jnp.where(_head_lanes(e), kf, 0.0).astype(BF16)
        vbufs[e][LEFT:LEFT + TS, :] = jnp.where(_head_lanes(e), vf, 1.0).astype(BF16)
    ubuf[SUBLANES:SUBLANES + TS, :] = _dot(h, w_in_ref[:, 3 * D_ATT:3 * D_ATT + D_LRU])
    gbuf[...] = _dot(h, w_in_ref[:, 3 * D_ATT + D_LRU:D_IN])

    def rows_of(i):
        return slice(i * SLAB, (i + 1) * SLAB)

    def conv_slab(i):
        xc = conv_b_ref[...]
        for j in range(CONV_W):
            off = i * SLAB + SUBLANES - (CONV_W - 1) + j
            xc = xc + ubuf[off:off + SLAB, :] * conv_w_ref[j:j + 1, :]
        xc_s[rows_of(i), :] = xc
        xcb_s[rows_of(i), :] = xc.astype(BF16)

    def gate_piece(c):
        gate, half = divmod(c, 2)
        cols = slice(half * MXU_N, (half + 1) * MXU_N)
        w_ref, dst = ((w_rg_ref, pr_s), (w_ig_ref, pi_s))[gate]
        dst[:, cols] = _dot(xcb_s[:, cols], w_ref[half])

    neg_l = -lru_l_ref[...]
    decay_rate = -LRU_C * (jnp.maximum(neg_l, 0.0) + jnp.log1p(jnp.exp(-jnp.abs(neg_l))))

    def coef_slab(i):
        r = jax.nn.sigmoid(pr_s[rows_of(i), :] + b_rg_ref[...])
        ig = jax.nn.sigmoid(pi_s[rows_of(i), :] + b_ig_ref[...])
        log_a = decay_rate * r
        a = jnp.exp(log_a)
        z = jnp.maximum(-jnp.tanh(log_a) * (a * a + 1.0), 0.0)
        mult = jnp.where(z > 0.0, z * lax.rsqrt(z), 0.0)
        a_s[rows_of(i), :] = a
        b_s[rows_of(i), :] = mult * (ig * xc_s[rows_of(i), :])

    def scan_slab(i, carry):
        groups = SLAB // SUBLANES
        a = a_s[rows_of(i), :].reshape(groups, SUBLANES, D_LRU)
        b = b_s[rows_of(i), :].reshape(groups, SUBLANES, D_LRU)
        row = lax.broadcasted_iota(jnp.int32, (1, SUBLANES, D_LRU), 1)
        for d in (1, 2, 4):
            keep = row >= d
            b = jnp.where(keep, a * pltpu.roll(b, d, axis=1) + b, b)
            a = jnp.where(keep, a * pltpu.roll(a, d, axis=1), a)
        for g in range(groups):
            hg = a[g] * carry + b[g]
            r0 = i * SLAB + g * SUBLANES
            hbuf[r0:r0 + SUBLANES, :] = hg
            carry = hg[SUBLANES - 1:SUBLANES, :]
        return carry

    def gated_slab(i):
        rec = hbuf[rows_of(i), :] * jax.nn.gelu(gbuf[rows_of(i), :])
        m_s[rows_of(i), D_ATT:D_MODEL] = _rmsnorm(rec, g_lru_ref[...]).astype(BF16)

    n_slabs = TS // SLAB
    n_blocks = (TS // QB) * ATT_HEADS
    conv_blocks = n_blocks // 4
    state = {"carry": hcar[0:1, :], "done": 0}

    def between(j):
        if j < conv_blocks:
            per = n_slabs // conv_blocks
            for i in range(j * per, (j + 1) * per):
                conv_slab(i)
            if j == conv_blocks - 1:
                for c in range(4):
                    gate_piece(c)
        else:
            upto = (j + 1 - conv_blocks) * n_slabs // (n_blocks - conv_blocks)
            for i in range(state["done"], upto):
                coef_slab(i)
                state["carry"] = scan_slab(i, state["carry"])
                gated_slab(i)
            state["done"] = upto

    _attention(q_s, kbufs, vbufs, bias_ref, abuf, between)
    hcar[0:1, :] = state["carry"]
    ubuf[0:SUBLANES, :] = ubuf[TS:TS + SUBLANES, :]
    for e in range(HEADS_PER_GROUP):
        kbufs[e][0:LEFT, :] = kbufs[e][TS:TS + LEFT, :]
        vbufs[e][0:LEFT, :] = vbufs[e][TS:TS + LEFT, :]

    m_s[:, 0:D_ATT] = _rmsnorm(abuf[...], g_att_ref[...]).astype(BF16)
    o_ref[...] = x + _dot(m_s[...], w_out_ref[...])


def _const_spec(shape):
    nd = len(shape)
    return pl.BlockSpec(shape, lambda *_: (0,) * nd, pipeline_mode=pl.Buffered(1))


def _mixer(x, g_mix, w_in, bias_tbl, conv_w, conv_b, w_rg, b_rg, w_ig, b_ig, lru_l,
           g_att, g_lru, w_out):
    B, S, D = x.shape
    tile = pl.BlockSpec((None, TS, D), lambda b, t: (b, t, 0))
    args = (g_mix, w_in, bias_tbl, conv_w, conv_b, w_rg, b_rg, w_ig, b_ig, lru_l,
            g_att, g_lru, w_out)
    kv = pltpu.VMEM((LEFT + TS, D_ATT), BF16)
    lru = pltpu.VMEM((TS, D_LRU), F32)
    return pl.pallas_call(
        _mixer_kernel,
        grid=(B, S // TS),
        in_specs=[tile] + [_const_spec(a.shape) for a in args],
        out_specs=tile,
        out_shape=jax.ShapeDtypeStruct((B, S, D), F32),
        scratch_shapes=[
            kv, kv,
            kv, kv,
            pltpu.VMEM((TS, D_ATT), BF16),
            pltpu.VMEM((SUBLANES + TS, D_LRU), F32),
            lru,
            lru,
            pltpu.VMEM((TS, D_LRU), BF16),
            lru, lru,
            lru, lru,
            lru,
            pltpu.VMEM((TS, D_ATT), F32),
            pltpu.VMEM((TS, D_MODEL), BF16),
            pltpu.VMEM((SUBLANES, D_LRU), F32),
        ],
        compiler_params=pltpu.CompilerParams(
            dimension_semantics=("arbitrary", "arbitrary"), vmem_limit_bytes=VMEM_LIMIT),
        name="mixer",
    )(x, *args)


def _cross_ffn_kernel(x_ref, kx_ref, vx_ref, g_cross_ref, wq_ref, wo_ref,
                      g_ffn_ref, w_gate_ref, w_up_ref, w_down_ref, g_final_ref, o_ref):
    x = x_ref[...]
    hc = _rmsnorm(x, g_cross_ref[...]).astype(BF16)
    q = _dot(hc, wq_ref[...]).astype(BF16)
    heads = []
    for hd in range(X_HEADS):
        cols = slice(hd * X_HEAD_DIM, (hd + 1) * X_HEAD_DIM)
        s = _dot_t(q[:, cols], kx_ref[:, cols])
        m = jnp.max(s, axis=-1, keepdims=True)
        p = jnp.exp(s - m)
        l = jnp.sum(p, axis=-1, keepdims=True)
        heads.append((_dot(p.astype(BF16), vx_ref[:, cols]) / l).astype(BF16))
    x = x + _dot(jnp.concatenate(heads, axis=1), wo_ref[...])

    hf = _rmsnorm(x, g_ffn_ref[...]).astype(BF16)
    y = x
    for lo, hi in FF_SPLITS:
        gate = _dot(hf, w_gate_ref[:, lo:hi])
        up = _dot(hf, w_up_ref[:, lo:hi])
        y = y + _dot((jax.nn.silu(gate) * up).astype(BF16), w_down_ref[lo:hi, :])
    o_ref[...] = _rmsnorm(y, g_final_ref[...])


def _cross_ffn(x, kx, vx, g_cross, wq, wo, g_ffn, w_gate, w_up, w_down, g_final):
    B, S, D = x.shape
    M = kx.shape[1]
    tile = pl.BlockSpec((None, TM, D), lambda b, t: (b, t, 0))
    memspec = pl.BlockSpec((None, M, D), lambda b, t: (b, 0, 0))
    args = (g_cross, wq, wo, g_ffn, w_gate, w_up, w_down, g_final)
    return pl.pallas_call(
        _cross_ffn_kernel,
        grid=(B, S // TM),
        in_specs=[tile, memspec, memspec] + [_const_spec(a.shape) for a in args],
        out_specs=tile,
        out_shape=jax.ShapeDtypeStruct((B, S, D), F32),
        compiler_params=pltpu.CompilerParams(
            dimension_semantics=("arbitrary", "arbitrary"), vmem_limit_bytes=VMEM_LIMIT),
        name="cross_ffn",
    )(x, kx, vx, *args)


def _bias_table(rel_bias):
    n_heads = rel_bias.shape[0]
    rb = rel_bias.astype(F32) * LOG2E
    slot = np.arange(CWIN) // CHUNK
    tbl = []
    for parity in range(2):
        rel_max = (parity + LEFT_CHUNKS) * CHUNK + (CHUNK - 1)
        rel_min = (parity + LEFT_CHUNKS) * CHUNK - (CWIN - 1)
        hi, lo = min(rel_max, MAX_REL), max(rel_min, -MAX_REL)
        vec = jnp.concatenate([
            jnp.broadcast_to(rb[:, 2 * MAX_REL:], (n_heads, rel_max - hi)),
            jnp.flip(rb[:, lo + MAX_REL:hi + MAX_REL + 1], axis=1),
            jnp.broadcast_to(rb[:, :1], (n_heads, lo - rel_min)),
        ], axis=1)
        rows = jnp.stack([vec[:, CHUNK - 1 - qi:CHUNK - 1 - qi + CWIN] for qi in range(CHUNK)],
                         axis=1)
        band = (slot >= parity) & (slot <= parity + LEFT_CHUNKS)
        tbl.append(jnp.where(band[None, None, :], rows, NEG))
    return jnp.stack(tbl, axis=1)


def _block_diag(w):
    per = LRU_BLOCKS // 2
    halves = []
    for hlf in range(2):
        rows = []
        for i in range(per):
            blocks = [w[hlf * per + i] if j == i else jnp.zeros((LRU_BLOCK, LRU_BLOCK), w.dtype)
                      for j in range(per)]
            rows.append(jnp.concatenate(blocks, axis=1))
        halves.append(jnp.concatenate(rows, axis=0))
    return jnp.stack(halves).astype(BF16)


def kernel(x, mem, g_mix, w_in, rel_bias, conv_w, conv_b, w_rg, b_rg, w_ig, b_ig, lru_L,
           g_out_attn, g_out_lru, w_out, g_cross, g_mem, wq_c, wk_c, wv_c, wo_c,
           g_ffn, w_gate, w_up, w_down, g_final):
    depth = g_mix.shape[0]
    row = lambda v: v.reshape(1, -1)
    bf = lambda w: w.astype(BF16)
    for l in range(depth):
        kx, vx = _mem_proj(mem, row(g_mem[l]), bf(wk_c[l]), bf(wv_c[l]))
        x = _mixer(x, row(g_mix[l]), bf(w_in[l]), _bias_table(rel_bias[l]),
                   conv_w[l], row(conv_b[l]), _block_diag(w_rg[l]), row(b_rg[l]),
                   _block_diag(w_ig[l]), row(b_ig[l]), row(lru_L[l]),
                   row(g_out_attn[l]), row(g_out_lru[l]), bf(w_out[l]))
        assert depth == 1
        x = _cross_ffn(x, kx, vx, row(g_cross[l]), bf(wq_c[l]), bf(wo_c[l]), row(g_ffn[l]),
                       bf(w_gate[l]), bf(w_up[l]), bf(w_down[l]), row(g_final))
    return x
```

```python
import functools

import numpy as np
import jax
import jax.numpy as jnp
from jax import lax
from jax.experimental import pallas as pl
from jax.experimental.pallas import tpu as pltpu

D_MODEL = 1024
CHUNK = 64
N_MEM = 256
ATT_HEADS = 8
HEAD_DIM = 64
D_ATT = ATT_HEADS * HEAD_DIM
D_LRU = D_MODEL - D_ATT
LRU_BLOCKS = 8
LRU_BLOCK = D_LRU // LRU_BLOCKS
CONV_W = 4
LRU_C = 8.0
LEFT_CHUNKS = 8
MAX_REL = 128
X_HEADS = 4
X_HEAD_DIM = D_MODEL // X_HEADS
D_FF = 2816
D_IN = 3 * D_ATT + 2 * D_LRU
EPS = 1e-6
NEG = -1e30
LOG2E = 1.4426950408889634

LANES = 128
SUBLANES = 8
LEFT = LEFT_CHUNKS * CHUNK
QB = 4 * CHUNK
KWIN = QB + LEFT
CWIN = 10 * CHUNK
HEADS_PER_GROUP = LANES // HEAD_DIM
N_GROUPS = D_ATT // LANES
TS = 512
SLAB = 32
MXU_N = 256
TM = 512
FF_SPLITS = ((0, 1024), (1024, 2048), (2048, D_FF))
VMEM_LIMIT = 56 * 1024 * 1024

BF16 = jnp.bfloat16
F32 = jnp.float32


def _rmsnorm(x, g):
    return x * lax.rsqrt(jnp.mean(x * x, axis=-1, keepdims=True) + EPS) * g


def _dot(a, b):
    return jnp.dot(a, b, preferred_element_type=F32)


def _dot_t(a, b):
    return lax.dot_general(a, b, (((1,), (1,)), ((), ())), preferred_element_type=F32)


def _mem_kernel(mem_ref, g_ref, wk_ref, wv_ref, kx_ref, vx_ref):
    mn = _rmsnorm(mem_ref[...], g_ref[...]).astype(BF16)
    kx_ref[...] = (_dot(mn, wk_ref[...]) * (X_HEAD_DIM ** -0.5 * LOG2E)).astype(BF16)
    vx_ref[...] = _dot(mn, wv_ref[...]).astype(BF16)


def _mem_proj(mem, g_mem, wk, wv):
    B, M, D = mem.shape
    const = lambda b: (0, 0)
    return pl.pallas_call(
        _mem_kernel,
        grid=(B,),
        in_specs=[
            pl.BlockSpec((None, M, D), lambda b: (b, 0, 0)),
            pl.BlockSpec((1, D), const),
            pl.BlockSpec((D, D), const),
            pl.BlockSpec((D, D), const),
        ],
        out_specs=[
            pl.BlockSpec((None, M, D), lambda b: (b, 0, 0)),
            pl.BlockSpec((None, M, D), lambda b: (b, 0, 0)),
        ],
        out_shape=[jax.ShapeDtypeStruct((B, M, D), BF16)] * 2,
        compiler_params=pltpu.CompilerParams(
            dimension_semantics=("arbitrary",), vmem_limit_bytes=VMEM_LIMIT),
        name="mem_proj",
    )(mem, g_mem, wk, wv)


def _head_mask(e):
    lane = lax.broadcasted_iota(jnp.int32, (1, LANES), 1)
    return (lane // HEAD_DIM) == e


_BIAS_BLOCKS = ((3, 4), (0, 3, 4))


def _scores(q_s, kbuf, qb, grp):
    lane = lax.broadcasted_iota(jnp.int32, (1, LANES), 1)
    flag = jnp.broadcast_to(jnp.where(lane == 0, 1.0, 0.0).astype(BF16), (QB, LANES))
    qg = q_s[qb * QB:(qb + 1) * QB, grp * LANES:(grp + 1) * LANES]
    lhs = jnp.concatenate(
        [jnp.concatenate([jnp.where(_head_mask(e), qg, jnp.zeros_like(qg)), flag], axis=1)
         for e in range(HEADS_PER_GROUP)], axis=0)
    return _dot_t(lhs, kbuf[qb * QB:qb * QB + KWIN, grp * 2 * LANES:(grp + 1) * 2 * LANES])


def _probs(s, bias_ref, grp):
    p_rows = []
    for e in range(HEADS_PER_GROUP):
        head = grp * HEADS_PER_GROUP + e
        for i in range(QB // CHUNK):
            w0 = (i // 2) * LANES
            rows = slice(e * QB + i * CHUNK, e * QB + (i + 1) * CHUNK)
            blocks = []
            for blk in range(CWIN // LANES):
                sb = s[rows, w0 + blk * LANES:w0 + (blk + 1) * LANES]
                if blk in _BIAS_BLOCKS[i % 2]:
                    sb = sb + bias_ref[head, i % 2, :, blk * LANES:(blk + 1) * LANES]
                blocks.append(sb)
            si = jnp.concatenate(blocks, axis=1)
            p = jnp.exp2(si - jnp.max(si, axis=-1, keepdims=True)).astype(BF16)
            pad = jnp.zeros((CHUNK, KWIN - CWIN), BF16)
            p_rows.append(jnp.concatenate([p, pad] if w0 == 0 else [pad, p], axis=1))
    return jnp.concatenate(p_rows, axis=0)


def _weighted_values(p, vbuf, qb, grp):
    o = _dot(p, vbuf[qb * QB:qb * QB + KWIN, grp * 2 * LANES:(grp + 1) * 2 * LANES])
    o = o[:, 0:LANES] / o[:, LANES:2 * LANES]
    return jnp.where(_head_mask(0), o[0:QB], o[QB:2 * QB])


def _attention(q_s, kbuf, vbuf, bias_ref, abuf, between):
    order = [(qb, grp) for qb in range(TS // QB) for grp in range(N_GROUPS)]
    s_next = _scores(q_s, kbuf, *order[0])
    for j, (qb, grp) in enumerate(order):
        s_cur = s_next
        if j + 1 < len(order):
            s_next = _scores(q_s, kbuf, *order[j + 1])
        p = _probs(s_cur, bias_ref, grp)
        abuf[qb * QB:(qb + 1) * QB, grp * LANES:(grp + 1) * LANES] = (
            _weighted_values(p, vbuf, qb, grp))
        between(j)


def _mixer_kernel(x_ref, g_mix_ref, w_in_ref, bias_ref, conv_w_ref, conv_b_ref,
                  w_rg_ref, b_rg_ref, w_ig_ref, b_ig_ref, lru_l_ref,
                  g_att_ref, g_lru_ref, w_out_ref,
                  o_ref,
                  kbuf, vbuf, q_s, ubuf, gbuf, xc_s, xcb_s, pr_s, pi_s,
                  a_s, b_s, hbuf, abuf, m_s, hcar):
    t = pl.program_id(1)

    @pl.when(t == 0)
    def _():
        lane = lax.broadcasted_iota(jnp.int32, (LEFT + TS, 2 * D_ATT), 1)
        row = lax.broadcasted_iota(jnp.int32, (LEFT + TS, 2 * D_ATT), 0)
        const_lanes = (lane // LANES) % 2 == 1
        kbuf[...] = jnp.where(const_lanes & (row < LEFT), NEG, 0.0).astype(BF16)
        vbuf[...] = jnp.where(const_lanes, 1.0, 0.0).astype(BF16)
        ubuf[0:SUBLANES, :] = jnp.zeros((SUBLANES, D_LRU), F32)
        hcar[...] = jnp.zeros_like(hcar)

    x = x_ref[...]
    h = _rmsnorm(x, g_mix_ref[...]).astype(BF16)
    q_s[...] = (_dot(h, w_in_ref[:, 0:D_ATT]) * (HEAD_DIM ** -0.5 * LOG2E)).astype(BF16)
    kf = _dot(h, w_in_ref[:, D_ATT:2 * D_ATT]).astype(BF16)
    vf = _dot(h, w_in_ref[:, 2 * D_ATT:3 * D_ATT]).astype(BF16)
    for grp in range(N_GROUPS):
        src = slice(grp * LANES, (grp + 1) * LANES)
        dst = slice(grp * 2 * LANES, grp * 2 * LANES + LANES)
        kbuf[LEFT:LEFT + TS, dst] = kf[:, src]
        vbuf[LEFT:LEFT + TS, dst] = vf[:, src]
    ubuf[SUBLANES:SUBLANES + TS, :] = _dot(h, w_in_ref[:, 3 * D_ATT:3 * D_ATT + D_LRU])
    gbuf[...] = _dot(h, w_in_ref[:, 3 * D_ATT + D_LRU:D_IN])

    def rows_of(i):
        return slice(i * SLAB, (i + 1) * SLAB)

    def conv_slab(i):
        xc = conv_b_ref[...]
        for j in range(CONV_W):
            off = i * SLAB + SUBLANES - (CONV_W - 1) + j
            xc = xc + ubuf[off:off + SLAB, :] * conv_w_ref[j:j + 1, :]
        xc_s[rows_of(i), :] = xc
        xcb_s[rows_of(i), :] = xc.astype(BF16)

    def gate_piece(c):
        gate, half = divmod(c, 2)
        cols = slice(half * MXU_N, (half + 1) * MXU_N)
        w_ref, dst = ((w_rg_ref, pr_s), (w_ig_ref, pi_s))[gate]
        dst[:, cols] = _dot(xcb_s[:, cols], w_ref[half])

    neg_l = -lru_l_ref[...]
    decay_rate = -LRU_C * (jnp.maximum(neg_l, 0.0) + jnp.log1p(jnp.exp(-jnp.abs(neg_l))))

    def coef_slab(i):
        r = jax.nn.sigmoid(pr_s[rows_of(i), :] + b_rg_ref[...])
        ig = jax.nn.sigmoid(pi_s[rows_of(i), :] + b_ig_ref[...])
        log_a = decay_rate * r
        a = jnp.exp(log_a)
        z = jnp.maximum(-jnp.tanh(log_a) * (a * a + 1.0), 0.0)
        mult = jnp.where(z > 0.0, z * lax.rsqrt(z), 0.0)
        a_s[rows_of(i), :] = a
        b_s[rows_of(i), :] = mult * (ig * xc_s[rows_of(i), :])

    def scan_slab(i, carry):
        groups = SLAB // SUBLANES
        a = a_s[rows_of(i), :].reshape(groups, SUBLANES, D_LRU)
        b = b_s[rows_of(i), :].reshape(groups, SUBLANES, D_LRU)
        row = lax.broadcasted_iota(jnp.int32, (1, SUBLANES, D_LRU), 1)
        for d in (1, 2, 4):
            keep = row >= d
            b = jnp.where(keep, a * pltpu.roll(b, d, axis=1) + b, b)
            a = jnp.where(keep, a * pltpu.roll(a, d, axis=1), a)
        for g in range(groups):
            hg = a[g] * carry + b[g]
            r0 = i * SLAB + g * SUBLANES
            hbuf[r0:r0 + SUBLANES, :] = hg
            carry = hg[SUBLANES - 1:SUBLANES, :]
        return carry

    def gated_slab(i):
        rec = hbuf[rows_of(i), :] * jax.nn.gelu(gbuf[rows_of(i), :])
        m_s[rows_of(i), D_ATT:D_MODEL] = _rmsnorm(rec, g_lru_ref[...]).astype(BF16)

    n_slabs = TS // SLAB
    n_blocks = (TS // QB) * N_GROUPS
    conv_blocks = n_blocks // 4
    state = {"carry": hcar[0:1, :], "done": 0}

    def between(j):
        if j < conv_blocks:
            per = n_slabs // conv_blocks
            for i in range(j * per, (j + 1) * per):
                conv_slab(i)
            if j == conv_blocks - 1:
                for c in range(4):
                    gate_piece(c)
        else:
            upto = (j + 1 - conv_blocks) * n_slabs // (n_blocks - conv_blocks)
            for i in range(state["done"], upto):
                coef_slab(i)
                state["carry"] = scan_slab(i, state["carry"])
                gated_slab(i)
            state["done"] = upto

    _attention(q_s, kbuf, vbuf, bias_ref, abuf, between)
    hcar[0:1, :] = state["carry"]
    ubuf[0:SUBLANES, :] = ubuf[TS:TS + SUBLANES, :]
    kbuf[0:LEFT, :] = kbuf[TS:TS + LEFT, :]
    vbuf[0:LEFT, :] = vbuf[TS:TS + LEFT, :]

    m_s[:, 0:D_ATT] = _rmsnorm(abuf[...], g_att_ref[...]).astype(BF16)
    o_ref[...] = x + _dot(m_s[...], w_out_ref[...])


def _const_spec(shape):
    nd = len(shape)
    return pl.BlockSpec(shape, lambda *_: (0,) * nd, pipeline_mode=pl.Buffered(1))


def _mixer(x, g_mix, w_in, bias_tbl, conv_w, conv_b, w_rg, b_rg, w_ig, b_ig, lru_l,
           g_att, g_lru, w_out):
    B, S, D = x.shape
    tile = pl.BlockSpec((None, TS, D), lambda b, t: (b, t, 0))
    args = (g_mix, w_in, bias_tbl, conv_w, conv_b, w_rg, b_rg, w_ig, b_ig, lru_l,
            g_att, g_lru, w_out)
    kv = pltpu.VMEM((LEFT + TS, 2 * D_ATT), BF16)
    lru = pltpu.VMEM((TS, D_LRU), F32)
    return pl.pallas_call(
        _mixer_kernel,
        grid=(B, S // TS),
        in_specs=[tile] + [_const_spec(a.shape) for a in args],
        out_specs=tile,
        out_shape=jax.ShapeDtypeStruct((B, S, D), F32),
        scratch_shapes=[
            kv,
            kv,
            pltpu.VMEM((TS, D_ATT), BF16),
            pltpu.VMEM((SUBLANES + TS, D_LRU), F32),
            lru,
            lru,
            pltpu.VMEM((TS, D_LRU), BF16),
            lru, lru,
            lru, lru,
            lru,
            pltpu.VMEM((TS, D_ATT), F32),
            pltpu.VMEM((TS, D_MODEL), BF16),
            pltpu.VMEM((SUBLANES, D_LRU), F32),
        ],
        compiler_params=pltpu.CompilerParams(
            dimension_semantics=("arbitrary", "arbitrary"), vmem_limit_bytes=VMEM_LIMIT),
        name="mixer",
    )(x, *args)


def _cross_ffn_kernel(x_ref, kx_ref, vx_ref, g_cross_ref, wq_ref, wo_ref,
                      g_ffn_ref, w_gate_ref, w_up_ref, w_down_ref, g_final_ref, o_ref):
    x = x_ref[...]
    hc = _rmsnorm(x, g_cross_ref[...]).astype(BF16)
    q = _dot(hc, wq_ref[...]).astype(BF16)

    def scores(hd):
        cols = slice(hd * X_HEAD_DIM, (hd + 1) * X_HEAD_DIM)
        return _dot_t(q[:, cols], kx_ref[:, cols])

    heads = []
    s_next = scores(0)
    for hd in range(X_HEADS):
        s = s_next
        if hd + 1 < X_HEADS:
            s_next = scores(hd + 1)
        cols = slice(hd * X_HEAD_DIM, (hd + 1) * X_HEAD_DIM)
        p = jnp.exp2(s - jnp.max(s, axis=-1, keepdims=True))
        l = jnp.sum(p, axis=-1, keepdims=True)
        heads.append((_dot(p.astype(BF16), vx_ref[:, cols]) / l).astype(BF16))
    x = x + _dot(jnp.concatenate(heads, axis=1), wo_ref[...])

    hf = _rmsnorm(x, g_ffn_ref[...]).astype(BF16)

    def gate_up(c):
        lo, hi = FF_SPLITS[c]
        return _dot(hf, w_gate_ref[:, lo:hi]), _dot(hf, w_up_ref[:, lo:hi])

    y = x
    nxt = gate_up(0)
    for c, (lo, hi) in enumerate(FF_SPLITS):
        gate, up = nxt
        if c + 1 < len(FF_SPLITS):
            nxt = gate_up(c + 1)
        y = y + _dot((jax.nn.silu(gate) * up).astype(BF16), w_down_ref[lo:hi, :])
    o_ref[...] = _rmsnorm(y, g_final_ref[...])


def _cross_ffn(x, kx, vx, g_cross, wq, wo, g_ffn, w_gate, w_up, w_down, g_final):
    B, S, D = x.shape
    M = kx.shape[1]
    tile = pl.BlockSpec((None, TM, D), lambda b, t: (b, t, 0))
    memspec = pl.BlockSpec((None, M, D), lambda b, t: (b, 0, 0))
    args = (g_cross, wq, wo, g_ffn, w_gate, w_up, w_down, g_final)
    return pl.pallas_call(
        _cross_ffn_kernel,
        grid=(B, S // TM),
        in_specs=[tile, memspec, memspec] + [_const_spec(a.shape) for a in args],
        out_specs=tile,
        out_shape=jax.ShapeDtypeStruct((B, S, D), F32),
        compiler_params=pltpu.CompilerParams(
            dimension_semantics=("arbitrary", "arbitrary"), vmem_limit_bytes=VMEM_LIMIT),
        name="cross_ffn",
    )(x, kx, vx, *args)


def _bias_table(rel_bias):
    n_heads = rel_bias.shape[0]
    rb = rel_bias.astype(F32) * LOG2E
    slot = np.arange(CWIN) // CHUNK
    tbl = []
    for parity in range(2):
        rel_max = (parity + LEFT_CHUNKS) * CHUNK + (CHUNK - 1)
        rel_min = (parity + LEFT_CHUNKS) * CHUNK - (CWIN - 1)
        hi, lo = min(rel_max, MAX_REL), max(rel_min, -MAX_REL)
        vec = jnp.concatenate([
            jnp.broadcast_to(rb[:, 2 * MAX_REL:], (n_heads, rel_max - hi)),
            jnp.flip(rb[:, lo + MAX_REL:hi + MAX_REL + 1], axis=1),
            jnp.broadcast_to(rb[:, :1], (n_heads, lo - rel_min)),
        ], axis=1)
        rows = jnp.stack([vec[:, CHUNK - 1 - qi:CHUNK - 1 - qi + CWIN] for qi in range(CHUNK)],
                         axis=1)
        band = (slot >= parity) & (slot <= parity + LEFT_CHUNKS)
        far = rb[:, 2 * MAX_REL:, None]
        tbl.append(jnp.where(band[None, None, :], rows - far, NEG))
    return jnp.stack(tbl, axis=1)


def _block_diag(w):
    per = LRU_BLOCKS // 2
    halves = []
    for hlf in range(2):
        rows = []
        for i in range(per):
            blocks = [w[hlf * per + i] if j == i else jnp.zeros((LRU_BLOCK, LRU_BLOCK), w.dtype)
                      for j in range(per)]
            rows.append(jnp.concatenate(blocks, axis=1))
        halves.append(jnp.concatenate(rows, axis=0))
    return jnp.stack(halves).astype(BF16)


def kernel(x, mem, g_mix, w_in, rel_bias, conv_w, conv_b, w_rg, b_rg, w_ig, b_ig, lru_L,
           g_out_attn, g_out_lru, w_out, g_cross, g_mem, wq_c, wk_c, wv_c, wo_c,
           g_ffn, w_gate, w_up, w_down, g_final):
    depth = g_mix.shape[0]
    row = lambda v: v.reshape(1, -1)
    bf = lambda w: w.astype(BF16)
    for l in range(depth):
        kx, vx = _mem_proj(mem, row(g_mem[l]), bf(wk_c[l]), bf(wv_c[l]))
        x = _mixer(x, row(g_mix[l]), bf(w_in[l]), _bias_table(rel_bias[l]),
                   conv_w[l], row(conv_b[l]), _block_diag(w_rg[l]), row(b_rg[l]),
                   _block_diag(w_ig[l]), row(b_ig[l]), row(lru_L[l]),
                   row(g_out_attn[l]), row(g_out_lru[l]), bf(w_out[l]))
        assert depth == 1
        x = _cross_ffn(x, kx, vx, row(g_cross[l]), bf(wq_c[l]), bf(wo_c[l]), row(g_ffn[l]),
                       bf(w_gate[l]), bf(w_up[l]), bf(w_down[l]), row(g_final))
    return x
```

```python
import functools

import numpy as np
import jax
import jax.numpy as jnp
from jax import lax
from jax.experimental import pallas as pl
from jax.experimental.pallas import tpu as pltpu

D_MODEL = 1024
CHUNK = 64
N_MEM = 256
ATT_HEADS = 8
HEAD_DIM = 64
D_ATT = ATT_HEADS * HEAD_DIM
D_LRU = D_MODEL - D_ATT
LRU_BLOCKS = 8
LRU_BLOCK = D_LRU // LRU_BLOCKS
CONV_W = 4
LRU_C = 8.0
LEFT_CHUNKS = 8
MAX_REL = 128
X_HEADS = 4
X_HEAD_DIM = D_MODEL // X_HEADS
D_FF = 2816
D_IN = 3 * D_ATT + 2 * D_LRU
EPS = 1e-6
NEG = -1e30
LOG2E = 1.4426950408889634

LANES = 128
SUBLANES = 8
LEFT = LEFT_CHUNKS * CHUNK
QB = 4 * CHUNK
KWIN = QB + LEFT
CWIN = 10 * CHUNK
HEADS_PER_GROUP = LANES // HEAD_DIM
N_GROUPS = D_ATT // LANES
TS = 512
SLAB = 32
MXU_N = 256
TM = 1024
FF_SPLITS = ((0, 1024), (1024, 2048), (2048, D_FF))
VMEM_LIMIT = 56 * 1024 * 1024

BF16 = jnp.bfloat16
F32 = jnp.float32


def _rmsnorm(x, g):
    return x * lax.rsqrt(jnp.mean(x * x, axis=-1, keepdims=True) + EPS) * g


def _dot(a, b):
    return jnp.dot(a, b, preferred_element_type=F32)


def _dot_t(a, b):
    return lax.dot_general(a, b, (((1,), (1,)), ((), ())), preferred_element_type=F32)


def _mem_kernel(mem_ref, g_ref, wk_ref, wv_ref, kx_ref, vx_ref):
    mn = _rmsnorm(mem_ref[...], g_ref[...]).astype(BF16)
    kx_ref[...] = (_dot(mn, wk_ref[...]) * (X_HEAD_DIM ** -0.5 * LOG2E)).astype(BF16)
    vx_ref[...] = _dot(mn, wv_ref[...]).astype(BF16)


def _mem_proj(mem, g_mem, wk, wv):
    B, M, D = mem.shape
    const = lambda b: (0, 0)
    return pl.pallas_call(
        _mem_kernel,
        grid=(B,),
        in_specs=[
            pl.BlockSpec((None, M, D), lambda b: (b, 0, 0)),
            pl.BlockSpec((1, D), const),
            pl.BlockSpec((D, D), const),
            pl.BlockSpec((D, D), const),
        ],
        out_specs=[
            pl.BlockSpec((None, M, D), lambda b: (b, 0, 0)),
            pl.BlockSpec((None, M, D), lambda b: (b, 0, 0)),
        ],
        out_shape=[jax.ShapeDtypeStruct((B, M, D), BF16)] * 2,
        compiler_params=pltpu.CompilerParams(
            dimension_semantics=("arbitrary",), vmem_limit_bytes=VMEM_LIMIT),
        name="mem_proj",
    )(mem, g_mem, wk, wv)


def _head_mask(e):
    lane = lax.broadcasted_iota(jnp.int32, (1, LANES), 1)
    return (lane // HEAD_DIM) == e


_BIAS_BLOCKS = ((3, 4), (0, 3, 4))


def _scores(q_s, kbuf, qb, grp):
    lane = lax.broadcasted_iota(jnp.int32, (1, LANES), 1)
    flag = jnp.broadcast_to(jnp.where(lane == 0, 1.0, 0.0).astype(BF16), (QB, LANES))
    qg = q_s[qb * QB:(qb + 1) * QB, grp * LANES:(grp + 1) * LANES]
    lhs = jnp.concatenate(
        [jnp.concatenate([jnp.where(_head_mask(e), qg, jnp.zeros_like(qg)), flag], axis=1)
         for e in range(HEADS_PER_GROUP)], axis=0)
    return _dot_t(lhs, kbuf[qb * QB:qb * QB + KWIN, grp * 2 * LANES:(grp + 1) * 2 * LANES])


def _probs(s, bias_ref, grp):
    p_rows = []
    for e in range(HEADS_PER_GROUP):
        head = grp * HEADS_PER_GROUP + e
        for i in range(QB // CHUNK):
            w0 = (i // 2) * LANES
            rows = slice(e * QB + i * CHUNK, e * QB + (i + 1) * CHUNK)
            blocks = []
            for blk in range(CWIN // LANES):
                sb = s[rows, w0 + blk * LANES:w0 + (blk + 1) * LANES]
                if blk in _BIAS_BLOCKS[i % 2]:
                    sb = sb + bias_ref[head, i % 2, :, blk * LANES:(blk + 1) * LANES]
                blocks.append(sb)
            si = jnp.concatenate(blocks, axis=1)
            p = jnp.exp2(si - jnp.max(si, axis=-1, keepdims=True)).astype(BF16)
            pad = jnp.zeros((CHUNK, KWIN - CWIN), BF16)
            p_rows.append(jnp.concatenate([p, pad] if w0 == 0 else [pad, p], axis=1))
    return jnp.concatenate(p_rows, axis=0)


def _weighted_values(p, vbuf, qb, grp):
    o = _dot(p, vbuf[qb * QB:qb * QB + KWIN, grp * 2 * LANES:(grp + 1) * 2 * LANES])
    o = o[:, 0:LANES] / o[:, LANES:2 * LANES]
    return jnp.where(_head_mask(0), o[0:QB], o[QB:2 * QB])


def _attention(q_s, kbuf, vbuf, bias_ref, abuf, between):
    order = [(qb, grp) for qb in range(TS // QB) for grp in range(N_GROUPS)]
    s_next = _scores(q_s, kbuf, *order[0])
    for j, (qb, grp) in enumerate(order):
        s_cur = s_next
        if j + 1 < len(order):
            s_next = _scores(q_s, kbuf, *order[j + 1])
        p = _probs(s_cur, bias_ref, grp)
        abuf[qb * QB:(qb + 1) * QB, grp * LANES:(grp + 1) * LANES] = (
            _weighted_values(p, vbuf, qb, grp))
        between(j)


def _mixer_kernel(x_ref, g_mix_ref, w_in_ref, bias_ref, conv_w_ref, conv_b_ref,
                  w_rg_ref, b_rg_ref, w_ig_ref, b_ig_ref, lru_l_ref,
                  g_att_ref, g_lru_ref, w_out_ref,
                  o_ref,
                  kbuf, vbuf, q_s, ubuf, gbuf, xc_s, xcb_s, pr_s, pi_s,
                  a_s, b_s, hbuf, abuf, m_s, hcar):
    t = pl.program_id(1)

    @pl.when(t == 0)
    def _():
        lane = lax.broadcasted_iota(jnp.int32, (LEFT + TS, 2 * D_ATT), 1)
        row = lax.broadcasted_iota(jnp.int32, (LEFT + TS, 2 * D_ATT), 0)
        const_lanes = (lane // LANES) % 2 == 1
        kbuf[...] = jnp.where(const_lanes & (row < LEFT), NEG, 0.0).astype(BF16)
        vbuf[...] = jnp.where(const_lanes, 1.0, 0.0).astype(BF16)
        ubuf[0:SUBLANES, :] = jnp.zeros((SUBLANES, D_LRU), F32)
        hcar[...] = jnp.zeros_like(hcar)

    x = x_ref[...]
    h = _rmsnorm(x, g_mix_ref[...]).astype(BF16)
    q_s[...] = (_dot(h, w_in_ref[:, 0:D_ATT]) * (HEAD_DIM ** -0.5 * LOG2E)).astype(BF16)
    kf = _dot(h, w_in_ref[:, D_ATT:2 * D_ATT]).astype(BF16)
    vf = _dot(h, w_in_ref[:, 2 * D_ATT:3 * D_ATT]).astype(BF16)
    for grp in range(N_GROUPS):
        src = slice(grp * LANES, (grp + 1) * LANES)
        dst = slice(grp * 2 * LANES, grp * 2 * LANES + LANES)
        kbuf[LEFT:LEFT + TS, dst] = kf[:, src]
        vbuf[LEFT:LEFT + TS, dst] = vf[:, src]
    ubuf[SUBLANES:SUBLANES + TS, :] = _dot(h, w_in_ref[:, 3 * D_ATT:3 * D_ATT + D_LRU])
    gbuf[...] = _dot(h, w_in_ref[:, 3 * D_ATT + D_LRU:D_IN])

    def rows_of(i):
        return slice(i * SLAB, (i + 1) * SLAB)

    def conv_slab(i):
        xc = conv_b_ref[...]
        for j in range(CONV_W):
            off = i * SLAB + SUBLANES - (CONV_W - 1) + j
            xc = xc + ubuf[off:off + SLAB, :] * conv_w_ref[j:j + 1, :]
        xc_s[rows_of(i), :] = xc
        xcb_s[rows_of(i), :] = xc.astype(BF16)

    def gate_piece(c):
        gate, half = divmod(c, 2)
        cols = slice(half * MXU_N, (half + 1) * MXU_N)
        w_ref, dst = ((w_rg_ref, pr_s), (w_ig_ref, pi_s))[gate]
        dst[:, cols] = _dot(xcb_s[:, cols], w_ref[half])

    neg_l = -lru_l_ref[...]
    decay_rate = -LRU_C * (jnp.maximum(neg_l, 0.0) + jnp.log1p(jnp.exp(-jnp.abs(neg_l))))

    def coef_slab(i):
        r = jax.nn.sigmoid(pr_s[rows_of(i), :] + b_rg_ref[...])
        ig = jax.nn.sigmoid(pi_s[rows_of(i), :] + b_ig_ref[...])
        log_a = decay_rate * r
        a = jnp.exp(log_a)
        z = jnp.maximum(-jnp.tanh(log_a) * (a * a + 1.0), 0.0)
        mult = jnp.where(z > 0.0, z * lax.rsqrt(z), 0.0)
        a_s[rows_of(i), :] = a
        b_s[rows_of(i), :] = mult * (ig * xc_s[rows_of(i), :])

    def scan_slab(i, carry):
        groups = SLAB // SUBLANES
        a = a_s[rows_of(i), :].reshape(groups, SUBLANES, D_LRU)
        b = b_s[rows_of(i), :].reshape(groups, SUBLANES, D_LRU)
        row = lax.broadcasted_iota(jnp.int32, (1, SUBLANES, D_LRU), 1)
        for d in (1, 2, 4):
            keep = row >= d
            b = jnp.where(keep, a * pltpu.roll(b, d, axis=1) + b, b)
            a = jnp.where(keep, a * pltpu.roll(a, d, axis=1), a)
        for g in range(groups):
            hg = a[g] * carry + b[g]
            r0 = i * SLAB + g * SUBLANES
            hbuf[r0:r0 + SUBLANES, :] = hg
            carry = hg[SUBLANES - 1:SUBLANES, :]
        return carry

    def gated_slab(i):
        rec = hbuf[rows_of(i), :] * jax.nn.gelu(gbuf[rows_of(i), :])
        m_s[rows_of(i), D_ATT:D_MODEL] = _rmsnorm(rec, g_lru_ref[...]).astype(BF16)

    n_slabs = TS // SLAB
    n_blocks = (TS // QB) * N_GROUPS
    conv_blocks = n_blocks // 4
    state = {"carry": hcar[0:1, :], "done": 0}

    def between(j):
        if j < conv_blocks:
            per = n_slabs // conv_blocks
            for i in range(j * per, (j + 1) * per):
                conv_slab(i)
            if j == conv_blocks - 1:
                for c in range(4):
                    gate_piece(c)
        else:
            upto = (j + 1 - conv_blocks) * n_slabs // (n_blocks - conv_blocks)
            for i in range(state["done"], upto):
                coef_slab(i)
                state["carry"] = scan_slab(i, state["carry"])
                gated_slab(i)
            state["done"] = upto

    _attention(q_s, kbuf, vbuf, bias_ref, abuf, between)
    hcar[0:1, :] = state["carry"]
    ubuf[0:SUBLANES, :] = ubuf[TS:TS + SUBLANES, :]
    kbuf[0:LEFT, :] = kbuf[TS:TS + LEFT, :]
    vbuf[0:LEFT, :] = vbuf[TS:TS + LEFT, :]

    m_s[:, 0:D_ATT] = _rmsnorm(abuf[...], g_att_ref[...]).astype(BF16)
    o_ref[...] = x + _dot(m_s[...], w_out_ref[...])


def _const_spec(shape):
    nd = len(shape)
    return pl.BlockSpec(shape, lambda *_: (0,) * nd, pipeline_mode=pl.Buffered(1))


def _mixer(x, g_mix, w_in, bias_tbl, conv_w, conv_b, w_rg, b_rg, w_ig, b_ig, lru_l,
           g_att, g_lru, w_out):
    B, S, D = x.shape
    tile = pl.BlockSpec((None, TS, D), lambda b, t: (b, t, 0))
    args = (g_mix, w_in, bias_tbl, conv_w, conv_b, w_rg, b_rg, w_ig, b_ig, lru_l,
            g_att, g_lru, w_out)
    kv = pltpu.VMEM((LEFT + TS, 2 * D_ATT), BF16)
    lru = pltpu.VMEM((TS, D_LRU), F32)
    return pl.pallas_call(
        _mixer_kernel,
        grid=(B, S // TS),
        in_specs=[tile] + [_const_spec(a.shape) for a in args],
        out_specs=tile,
        out_shape=jax.ShapeDtypeStruct((B, S, D), F32),
        scratch_shapes=[
            kv,
            kv,
            pltpu.VMEM((TS, D_ATT), BF16),
            pltpu.VMEM((SUBLANES + TS, D_LRU), F32),
            lru,
            lru,
            pltpu.VMEM((TS, D_LRU), BF16),
            lru, lru,
            lru, lru,
            lru,
            pltpu.VMEM((TS, D_ATT), F32),
            pltpu.VMEM((TS, D_MODEL), BF16),
            pltpu.VMEM((SUBLANES, D_LRU), F32),
        ],
        compiler_params=pltpu.CompilerParams(
            dimension_semantics=("arbitrary", "arbitrary"), vmem_limit_bytes=VMEM_LIMIT),
        name="mixer",
    )(x, *args)


def _cross_ffn_kernel(x_ref, kx_ref, vx_ref, g_cross_ref, wq_ref, wo_ref,
                      g_ffn_ref, w_gate_ref, w_up_ref, w_down_ref, g_final_ref, o_ref):
    def half(rows):
        x = x_ref[rows, :]
        hc = _rmsnorm(x, g_cross_ref[...]).astype(BF16)
        yield
        q = _dot(hc, wq_ref[...]).astype(BF16)
        yield

        def scores(hd):
            cols = slice(hd * X_HEAD_DIM, (hd + 1) * X_HEAD_DIM)
            return _dot_t(q[:, cols], kx_ref[:, cols])

        heads = []
        for hd in range(X_HEADS):
            cols = slice(hd * X_HEAD_DIM, (hd + 1) * X_HEAD_DIM)
            s = scores(hd)
            yield
            p = jnp.exp2(s - jnp.max(s, axis=-1, keepdims=True))
            l = jnp.sum(p, axis=-1, keepdims=True)
            yield
            heads.append((_dot(p.astype(BF16), vx_ref[:, cols]) / l).astype(BF16))
        x = x + _dot(jnp.concatenate(heads, axis=1), wo_ref[...])
        yield
        hf = _rmsnorm(x, g_ffn_ref[...]).astype(BF16)
        yield
        y = x
        for lo, hi in FF_SPLITS:
            gate = _dot(hf, w_gate_ref[:, lo:hi])
            up = _dot(hf, w_up_ref[:, lo:hi])
            yield
            act = (jax.nn.silu(gate) * up).astype(BF16)
            yield
            y = y + _dot(act, w_down_ref[lo:hi, :])
        yield
        o_ref[rows, :] = _rmsnorm(y, g_final_ref[...])

    n_half = 2
    rows_per = TM // n_half
    gens = [half(slice(k * rows_per, (k + 1) * rows_per)) for k in range(n_half)]
    next(gens[0])
    live = list(gens)
    while live:
        for g in list(live):
            try:
                next(g)
            except StopIteration:
                live.remove(g)


def _cross_ffn(x, kx, vx, g_cross, wq, wo, g_ffn, w_gate, w_up, w_down, g_final):
    B, S, D = x.shape
    M = kx.shape[1]
    tile = pl.BlockSpec((None, TM, D), lambda b, t: (b, t, 0))
    memspec = pl.BlockSpec((None, M, D), lambda b, t: (b, 0, 0))
    args = (g_cross, wq, wo, g_ffn, w_gate, w_up, w_down, g_final)
    return pl.pallas_call(
        _cross_ffn_kernel,
        grid=(B, S // TM),
        in_specs=[tile, memspec, memspec] + [_const_spec(a.shape) for a in args],
        out_specs=tile,
        out_shape=jax.ShapeDtypeStruct((B, S, D), F32),
        compiler_params=pltpu.CompilerParams(
            dimension_semantics=("arbitrary", "arbitrary"), vmem_limit_bytes=VMEM_LIMIT),
        name="cross_ffn",
    )(x, kx, vx, *args)


def _bias_table(rel_bias):
    n_heads = rel_bias.shape[0]
    rb = rel_bias.astype(F32) * LOG2E
    slot = np.arange(CWIN) // CHUNK
    tbl = []
    for parity in range(2):
        rel_max = (parity + LEFT_CHUNKS) * CHUNK + (CHUNK - 1)
        rel_min = (parity + LEFT_CHUNKS) * CHUNK - (CWIN - 1)
        hi, lo = min(rel_max, MAX_REL), max(rel_min, -MAX_REL)
        vec = jnp.concatenate([
            jnp.broadcast_to(rb[:, 2 * MAX_REL:], (n_heads, rel_max - hi)),
            jnp.flip(rb[:, lo + MAX_REL:hi + MAX_REL + 1], axis=1),
            jnp.broadcast_to(rb[:, :1], (n_heads, lo - rel_min)),
        ], axis=1)
        rows = jnp.stack([vec[:, CHUNK - 1 - qi:CHUNK - 1 - qi + CWIN] for qi in range(CHUNK)],
                         axis=1)
        band = (slot >= parity) & (slot <= parity + LEFT_CHUNKS)
        far = rb[:, 2 * MAX_REL:, None]
        tbl.append(jnp.where(band[None, None, :], rows - far, NEG))
    return jnp.stack(tbl, axis=1)


def _block_diag(w):
    per = LRU_BLOCKS // 2
    halves = []
    for hlf in range(2):
        rows = []
        for i in range(per):
            blocks = [w[hlf * per + i] if j == i else jnp.zeros((LRU_BLOCK, LRU_BLOCK), w.dtype)
                      for j in range(per)]
            rows.append(jnp.concatenate(blocks, axis=1))
        halves.append(jnp.concatenate(rows, axis=0))
    return jnp.stack(halves).astype(BF16)


def kernel(x, mem, g_mix, w_in, rel_bias, conv_w, conv_b, w_rg, b_rg, w_ig, b_ig, lru_L,
           g_out_attn, g_out_lru, w_out, g_cross, g_mem, wq_c, wk_c, wv_c, wo_c,
           g_ffn, w_gate, w_up, w_down, g_final):
    depth = g_mix.shape[0]
    row = lambda v: v.reshape(1, -1)
    bf = lambda w: w.astype(BF16)
    for l in range(depth):
        kx, vx = _mem_proj(mem, row(g_mem[l]), bf(wk_c[l]), bf(wv_c[l]))
        x = _mixer(x, row(g_mix[l]), bf(w_in[l]), _bias_table(rel_bias[l]),
                   conv_w[l], row(conv_b[l]), _block_diag(w_rg[l]), row(b_rg[l]),
                   _block_diag(w_ig[l]), row(b_ig[l]), row(lru_L[l]),
                   row(g_out_attn[l]), row(g_out_lru[l]), bf(w_out[l]))
        assert depth == 1
        x = _cross_ffn(x, kx, vx, row(g_cross[l]), bf(wq_c[l]), bf(wo_c[l]), row(g_ffn[l]),
                       bf(w_gate[l]), bf(w_up[l]), bf(w_down[l]), row(g_final))
    return x
```

```python
import functools

import numpy as np
import jax
import jax.numpy as jnp
from jax import lax
from jax.experimental import pallas as pl
from jax.experimental.pallas import tpu as pltpu

D_MODEL = 1024
CHUNK = 64
N_MEM = 256
ATT_HEADS = 8
HEAD_DIM = 64
D_ATT = ATT_HEADS * HEAD_DIM
D_LRU = D_MODEL - D_ATT
LRU_BLOCKS = 8
LRU_BLOCK = D_LRU // LRU_BLOCKS
CONV_W = 4
LRU_C = 8.0
LEFT_CHUNKS = 8
MAX_REL = 128
X_HEADS = 4
X_HEAD_DIM = D_MODEL // X_HEADS
D_FF = 2816
D_IN = 3 * D_ATT + 2 * D_LRU
EPS = 1e-6
NEG = -1e30
LOG2E = 1.4426950408889634

LANES = 128
SUBLANES = 8
LEFT = LEFT_CHUNKS * CHUNK
QB = 4 * CHUNK
KWIN = QB + LEFT
CWIN = 10 * CHUNK
HEADS_PER_GROUP = LANES // HEAD_DIM
N_GROUPS = D_ATT // LANES
TS = 512
SLAB = 128
MXU_N = 256
TM = 1024
FF_SPLITS = ((0, 1024), (1024, 2048), (2048, D_FF))
VMEM_LIMIT = 56 * 1024 * 1024

BF16 = jnp.bfloat16
F32 = jnp.float32


def _rmsnorm(x, g):
    return x * lax.rsqrt(jnp.mean(x * x, axis=-1, keepdims=True) + EPS) * g


def _dot(a, b):
    return jnp.dot(a, b, preferred_element_type=F32)


def _dot_t(a, b):
    return lax.dot_general(a, b, (((1,), (1,)), ((), ())), preferred_element_type=F32)


def _mem_kernel(mem_ref, g_ref, wk_ref, wv_ref, kx_ref, vx_ref):
    mn = _rmsnorm(mem_ref[...], g_ref[...]).astype(BF16)
    kx_ref[...] = (_dot(mn, wk_ref[...]) * (X_HEAD_DIM ** -0.5 * LOG2E)).astype(BF16)
    vx_ref[...] = _dot(mn, wv_ref[...]).astype(BF16)


def _mem_proj(mem, g_mem, wk, wv):
    B, M, D = mem.shape
    const = lambda b: (0, 0)
    return pl.pallas_call(
        _mem_kernel,
        grid=(B,),
        in_specs=[
            pl.BlockSpec((None, M, D), lambda b: (b, 0, 0)),
            pl.BlockSpec((1, D), const),
            pl.BlockSpec((D, D), const),
            pl.BlockSpec((D, D), const),
        ],
        out_specs=[
            pl.BlockSpec((None, M, D), lambda b: (b, 0, 0)),
            pl.BlockSpec((None, M, D), lambda b: (b, 0, 0)),
        ],
        out_shape=[jax.ShapeDtypeStruct((B, M, D), BF16)] * 2,
        compiler_params=pltpu.CompilerParams(
            dimension_semantics=("arbitrary",), vmem_limit_bytes=VMEM_LIMIT),
        name="mem_proj",
    )(mem, g_mem, wk, wv)


def _head_mask(e):
    lane = lax.broadcasted_iota(jnp.int32, (1, LANES), 1)
    return (lane // HEAD_DIM) == e


_BIAS_BLOCKS = ((3, 4), (0, 3, 4))


def _scores(q_s, kbuf, qb, grp):
    lane = lax.broadcasted_iota(jnp.int32, (1, LANES), 1)
    flag = jnp.broadcast_to(jnp.where(lane == 0, 1.0, 0.0).astype(BF16), (QB, LANES))
    qg = q_s[qb * QB:(qb + 1) * QB, grp * LANES:(grp + 1) * LANES]
    lhs = jnp.concatenate(
        [jnp.concatenate([jnp.where(_head_mask(e), qg, jnp.zeros_like(qg)), flag], axis=1)
         for e in range(HEADS_PER_GROUP)], axis=0)
    return _dot_t(lhs, kbuf[qb * QB:qb * QB + KWIN, grp * 2 * LANES:(grp + 1) * 2 * LANES])


def _probs(s, bias_ref, grp):
    p_rows = []
    for e in range(HEADS_PER_GROUP):
        head = grp * HEADS_PER_GROUP + e
        for i in range(QB // CHUNK):
            w0 = (i // 2) * LANES
            rows = slice(e * QB + i * CHUNK, e * QB + (i + 1) * CHUNK)
            blocks = []
            for blk in range(CWIN // LANES):
                sb = s[rows, w0 + blk * LANES:w0 + (blk + 1) * LANES]
                if blk in _BIAS_BLOCKS[i % 2]:
                    sb = sb + bias_ref[head, i % 2, :, blk * LANES:(blk + 1) * LANES]
                blocks.append(sb)
            si = jnp.concatenate(blocks, axis=1)
            p = jnp.exp2(si - jnp.max(si, axis=-1, keepdims=True)).astype(BF16)
            pad = jnp.zeros((CHUNK, KWIN - CWIN), BF16)
            p_rows.append(jnp.concatenate([p, pad] if w0 == 0 else [pad, p], axis=1))
    return jnp.concatenate(p_rows, axis=0)


def _weighted_values(p, vbuf, qb, grp):
    o = _dot(p, vbuf[qb * QB:qb * QB + KWIN, grp * 2 * LANES:(grp + 1) * 2 * LANES])
    o = o[:, 0:LANES] / o[:, LANES:2 * LANES]
    return jnp.where(_head_mask(0), o[0:QB], o[QB:2 * QB])


def _attention(q_s, kbuf, vbuf, bias_ref, abuf, between):
    order = [(qb, grp) for qb in range(TS // QB) for grp in range(N_GROUPS)]
    s_next = _scores(q_s, kbuf, *order[0])
    for j, (qb, grp) in enumerate(order):
        s_cur = s_next
        if j + 1 < len(order):
            s_next = _scores(q_s, kbuf, *order[j + 1])
        p = _probs(s_cur, bias_ref, grp)
        abuf[qb * QB:(qb + 1) * QB, grp * LANES:(grp + 1) * LANES] = (
            _weighted_values(p, vbuf, qb, grp))
        between(j)


def _mixer_kernel(x_ref, g_mix_ref, w_in_ref, bias_ref, conv_w_ref, conv_b_ref,
                  w_rg_ref, b_rg_ref, w_ig_ref, b_ig_ref, lru_l_ref,
                  g_att_ref, g_lru_ref, w_out_ref,
                  o_ref,
                  kbuf, vbuf, q_s, ubuf, gbuf, xc_s, xcb_s, pr_s, pi_s,
                  a_s, b_s, hbuf, abuf, m_s, hcar):
    t = pl.program_id(1)

    @pl.when(t == 0)
    def _():
        lane = lax.broadcasted_iota(jnp.int32, (LEFT + TS, 2 * D_ATT), 1)
        row = lax.broadcasted_iota(jnp.int32, (LEFT + TS, 2 * D_ATT), 0)
        const_lanes = (lane // LANES) % 2 == 1
        kbuf[...] = jnp.where(const_lanes & (row < LEFT), NEG, 0.0).astype(BF16)
        vbuf[...] = jnp.where(const_lanes, 1.0, 0.0).astype(BF16)
        ubuf[0:SUBLANES, :] = jnp.zeros((SUBLANES, D_LRU), F32)
        hcar[...] = jnp.zeros_like(hcar)

    x = x_ref[...]
    h = _rmsnorm(x, g_mix_ref[...]).astype(BF16)
    q_s[...] = (_dot(h, w_in_ref[:, 0:D_ATT]) * (HEAD_DIM ** -0.5 * LOG2E)).astype(BF16)
    kf = _dot(h, w_in_ref[:, D_ATT:2 * D_ATT]).astype(BF16)
    vf = _dot(h, w_in_ref[:, 2 * D_ATT:3 * D_ATT]).astype(BF16)
    for grp in range(N_GROUPS):
        src = slice(grp * LANES, (grp + 1) * LANES)
        dst = slice(grp * 2 * LANES, grp * 2 * LANES + LANES)
        kbuf[LEFT:LEFT + TS, dst] = kf[:, src]
        vbuf[LEFT:LEFT + TS, dst] = vf[:, src]
    ubuf[SUBLANES:SUBLANES + TS, :] = _dot(h, w_in_ref[:, 3 * D_ATT:3 * D_ATT + D_LRU])
    gbuf[...] = _dot(h, w_in_ref[:, 3 * D_ATT + D_LRU:D_IN])

    def rows_of(i):
        return slice(i * SLAB, (i + 1) * SLAB)

    def conv_slab(i):
        xc = conv_b_ref[...]
        for j in range(CONV_W):
            off = i * SLAB + SUBLANES - (CONV_W - 1) + j
            xc = xc + ubuf[off:off + SLAB, :] * conv_w_ref[j:j + 1, :]
        xc_s[rows_of(i), :] = xc
        xcb_s[rows_of(i), :] = xc.astype(BF16)

    def gate_piece(c):
        gate, half = divmod(c, 2)
        cols = slice(half * MXU_N, (half + 1) * MXU_N)
        w_ref, dst = ((w_rg_ref, pr_s), (w_ig_ref, pi_s))[gate]
        dst[:, cols] = _dot(xcb_s[:, cols], w_ref[half])

    neg_l = -lru_l_ref[...]
    decay_rate = (-LRU_C * LOG2E) * (jnp.maximum(neg_l, 0.0) + jnp.log1p(jnp.exp(-jnp.abs(neg_l))))

    def coef_slab(i):
        r = pl.reciprocal(1.0 + jnp.exp2(pr_s[rows_of(i), :] + b_rg_ref[...]))
        ig = pl.reciprocal(1.0 + jnp.exp2(pi_s[rows_of(i), :] + b_ig_ref[...]))
        a = jnp.exp2(decay_rate * r)
        z = jnp.maximum(1.0 - a * a, 0.0)
        mult = jnp.where(z > 0.0, z * lax.rsqrt(z), 0.0)
        a_s[rows_of(i), :] = a
        b_s[rows_of(i), :] = mult * (ig * xc_s[rows_of(i), :])

    def scan_slab(i, carry):
        groups = SLAB // SUBLANES
        a = a_s[rows_of(i), :].reshape(groups, SUBLANES, D_LRU)
        b = b_s[rows_of(i), :].reshape(groups, SUBLANES, D_LRU)
        row = lax.broadcasted_iota(jnp.int32, (1, SUBLANES, D_LRU), 1)
        for d in (1, 2, 4):
            keep = row >= d
            b = jnp.where(keep, a * pltpu.roll(b, d, axis=1) + b, b)
            a = jnp.where(keep, a * pltpu.roll(a, d, axis=1), a)
        for g in range(groups):
            hg = a[g] * carry + b[g]
            r0 = i * SLAB + g * SUBLANES
            hbuf[r0:r0 + SUBLANES, :] = hg
            carry = hg[SUBLANES - 1:SUBLANES, :]
        return carry

    def gated_slab(i):
        rec = hbuf[rows_of(i), :] * jax.nn.gelu(gbuf[rows_of(i), :])
        m_s[rows_of(i), D_ATT:D_MODEL] = _rmsnorm(rec, g_lru_ref[...]).astype(BF16)

    n_slabs = TS // SLAB
    n_blocks = (TS // QB) * N_GROUPS
    conv_blocks = n_blocks // 4
    state = {"carry": hcar[0:1, :], "done": 0}

    def between(j):
        if j < conv_blocks:
            per = n_slabs // conv_blocks
            for i in range(j * per, (j + 1) * per):
                conv_slab(i)
            if j == conv_blocks - 1:
                for c in range(4):
                    gate_piece(c)
        else:
            upto = (j + 1 - conv_blocks) * n_slabs // (n_blocks - conv_blocks)
            for i in range(state["done"], upto):
                coef_slab(i)
                state["carry"] = scan_slab(i, state["carry"])
                gated_slab(i)
            state["done"] = upto

    _attention(q_s, kbuf, vbuf, bias_ref, abuf, between)
    hcar[0:1, :] = state["carry"]
    ubuf[0:SUBLANES, :] = ubuf[TS:TS + SUBLANES, :]
    kbuf[0:LEFT, :] = kbuf[TS:TS + LEFT, :]
    vbuf[0:LEFT, :] = vbuf[TS:TS + LEFT, :]

    m_s[:, 0:D_ATT] = _rmsnorm(abuf[...], g_att_ref[...]).astype(BF16)
    o_ref[...] = x + _dot(m_s[...], w_out_ref[...])


def _const_spec(shape):
    nd = len(shape)
    return pl.BlockSpec(shape, lambda *_: (0,) * nd, pipeline_mode=pl.Buffered(1))


def _mixer(x, g_mix, w_in, bias_tbl, conv_w, conv_b, w_rg, b_rg, w_ig, b_ig, lru_l,
           g_att, g_lru, w_out):
    B, S, D = x.shape
    tile = pl.BlockSpec((None, TS, D), lambda b, t: (b, t, 0))
    args = (g_mix, w_in, bias_tbl, conv_w, conv_b, w_rg, b_rg, w_ig, b_ig, lru_l,
            g_att, g_lru, w_out)
    kv = pltpu.VMEM((LEFT + TS, 2 * D_ATT), BF16)
    lru = pltpu.VMEM((TS, D_LRU), F32)
    return pl.pallas_call(
        _mixer_kernel,
        grid=(B, S // TS),
        in_specs=[tile] + [_const_spec(a.shape) for a in args],
        out_specs=tile,
        out_shape=jax.ShapeDtypeStruct((B, S, D), F32),
        scratch_shapes=[
            kv,
            kv,
            pltpu.VMEM((TS, D_ATT), BF16),
            pltpu.VMEM((SUBLANES + TS, D_LRU), F32),
            lru,
            lru,
            pltpu.VMEM((TS, D_LRU), BF16),
            lru, lru,
            lru, lru,
            lru,
            pltpu.VMEM((TS, D_ATT), F32),
            pltpu.VMEM((TS, D_MODEL), BF16),
            pltpu.VMEM((SUBLANES, D_LRU), F32),
        ],
        compiler_params=pltpu.CompilerParams(
            dimension_semantics=("arbitrary", "arbitrary"), vmem_limit_bytes=VMEM_LIMIT),
        name="mixer",
    )(x, *args)


def _cross_ffn_kernel(x_ref, kx_ref, vx_ref, g_cross_ref, wq_ref, wo_ref,
                      g_ffn_ref, w_gate_ref, w_up_ref, w_down_ref, g_final_ref, o_ref):
    def half(rows):
        x = x_ref[rows, :]
        hc = _rmsnorm(x, g_cross_ref[...]).astype(BF16)
        yield
        q = _dot(hc, wq_ref[...]).astype(BF16)
        yield

        def scores(hd):
            cols = slice(hd * X_HEAD_DIM, (hd + 1) * X_HEAD_DIM)
            return _dot_t(q[:, cols], kx_ref[:, cols])

        heads = []
        for hd in range(X_HEADS):
            cols = slice(hd * X_HEAD_DIM, (hd + 1) * X_HEAD_DIM)
            s = scores(hd)
            yield
            p = jnp.exp2(s - jnp.max(s, axis=-1, keepdims=True))
            l = jnp.sum(p, axis=-1, keepdims=True)
            yield
            heads.append((_dot(p.astype(BF16), vx_ref[:, cols]) / l).astype(BF16))
        x = x + _dot(jnp.concatenate(heads, axis=1), wo_ref[...])
        yield
        hf = _rmsnorm(x, g_ffn_ref[...]).astype(BF16)
        yield
        y = x
        for lo, hi in FF_SPLITS:
            gate = _dot(hf, w_gate_ref[:, lo:hi])
            up = _dot(hf, w_up_ref[:, lo:hi])
            yield
            act = (jax.nn.silu(gate) * up).astype(BF16)
            yield
            y = y + _dot(act, w_down_ref[lo:hi, :])
        yield
        o_ref[rows, :] = _rmsnorm(y, g_final_ref[...])

    n_half = 2
    rows_per = TM // n_half
    gens = [half(slice(k * rows_per, (k + 1) * rows_per)) for k in range(n_half)]
    next(gens[0])
    live = list(gens)
    while live:
        for g in list(live):
            try:
                next(g)
            except StopIteration:
                live.remove(g)


def _cross_ffn(x, kx, vx, g_cross, wq, wo, g_ffn, w_gate, w_up, w_down, g_final):
    B, S, D = x.shape
    M = kx.shape[1]
    tile = pl.BlockSpec((None, TM, D), lambda b, t: (b, t, 0))
    memspec = pl.BlockSpec((None, M, D), lambda b, t: (b, 0, 0))
    args = (g_cross, wq, wo, g_ffn, w_gate, w_up, w_down, g_final)
    return pl.pallas_call(
        _cross_ffn_kernel,
        grid=(B, S // TM),
        in_specs=[tile, memspec, memspec] + [_const_spec(a.shape) for a in args],
        out_specs=tile,
        out_shape=jax.ShapeDtypeStruct((B, S, D), F32),
        compiler_params=pltpu.CompilerParams(
            dimension_semantics=("arbitrary", "arbitrary"), vmem_limit_bytes=VMEM_LIMIT),
        name="cross_ffn",
    )(x, kx, vx, *args)


def _bias_table(rel_bias):
    n_heads = rel_bias.shape[0]
    rb = rel_bias.astype(F32) * LOG2E
    slot = np.arange(CWIN) // CHUNK
    tbl = []
    for parity in range(2):
        rel_max = (parity + LEFT_CHUNKS) * CHUNK + (CHUNK - 1)
        rel_min = (parity + LEFT_CHUNKS) * CHUNK - (CWIN - 1)
        hi, lo = min(rel_max, MAX_REL), max(rel_min, -MAX_REL)
        vec = jnp.concatenate([
            jnp.broadcast_to(rb[:, 2 * MAX_REL:], (n_heads, rel_max - hi)),
            jnp.flip(rb[:, lo + MAX_REL:hi + MAX_REL + 1], axis=1),
            jnp.broadcast_to(rb[:, :1], (n_heads, lo - rel_min)),
        ], axis=1)
        rows = jnp.stack([vec[:, CHUNK - 1 - qi:CHUNK - 1 - qi + CWIN] for qi in range(CHUNK)],
                         axis=1)
        band = (slot >= parity) & (slot <= parity + LEFT_CHUNKS)
        far = rb[:, 2 * MAX_REL:, None]
        tbl.append(jnp.where(band[None, None, :], rows - far, NEG))
    return jnp.stack(tbl, axis=1)


def _block_diag(w):
    per = LRU_BLOCKS // 2
    halves = []
    for hlf in range(2):
        rows = []
        for i in range(per):
            blocks = [w[hlf * per + i] if j == i else jnp.zeros((LRU_BLOCK, LRU_BLOCK), w.dtype)
                      for j in range(per)]
            rows.append(jnp.concatenate(blocks, axis=1))
        halves.append(jnp.concatenate(rows, axis=0))
    return jnp.stack(halves).astype(BF16)


def kernel(x, mem, g_mix, w_in, rel_bias, conv_w, conv_b, w_rg, b_rg, w_ig, b_ig, lru_L,
           g_out_attn, g_out_lru, w_out, g_cross, g_mem, wq_c, wk_c, wv_c, wo_c,
           g_ffn, w_gate, w_up, w_down, g_final):
    depth = g_mix.shape[0]
    row = lambda v: v.reshape(1, -1)
    bf = lambda w: w.astype(BF16)
    for l in range(depth):
        kx, vx = _mem_proj(mem, row(g_mem[l]), bf(wk_c[l]), bf(wv_c[l]))
        x = _mixer(x, row(g_mix[l]), bf(w_in[l]), _bias_table(rel_bias[l]),
                   conv_w[l], row(conv_b[l]), _block_diag(-LOG2E * w_rg[l]), row(-LOG2E * b_rg[l]),
                   _block_diag(-LOG2E * w_ig[l]), row(-LOG2E * b_ig[l]), row(lru_L[l]),
                   row(g_out_attn[l]), row(g_out_lru[l]), bf(w_out[l]))
        assert depth == 1
        x = _cross_ffn(x, kx, vx, row(g_cross[l]), bf(wq_c[l]), bf(wo_c[l]), row(g_ffn[l]),
                       bf(w_gate[l]), bf(w_up[l]), bf(w_down[l]), row(g_final))
    return x
```

```python
import functools

import numpy as np
import jax
import jax.numpy as jnp
from jax import lax
from jax.experimental import pallas as pl
from jax.experimental.pallas import tpu as pltpu

D_MODEL = 1024
CHUNK = 64
N_MEM = 256
ATT_HEADS = 8
HEAD_DIM = 64
D_ATT = ATT_HEADS * HEAD_DIM
D_LRU = D_MODEL - D_ATT
LRU_BLOCKS = 8
LRU_BLOCK = D_LRU // LRU_BLOCKS
CONV_W = 4
LRU_C = 8.0
LEFT_CHUNKS = 8
MAX_REL = 128
X_HEADS = 4
X_HEAD_DIM = D_MODEL // X_HEADS
D_FF = 2816
D_IN = 3 * D_ATT + 2 * D_LRU
EPS = 1e-6
NEG = -1e30
LOG2E = 1.4426950408889634

LANES = 128
SUBLANES = 8
LEFT = LEFT_CHUNKS * CHUNK
QB = 4 * CHUNK
KWIN = QB + LEFT
CWIN = 10 * CHUNK
HEADS_PER_GROUP = LANES // HEAD_DIM
N_GROUPS = D_ATT // LANES
TS = 512
SLAB = 128
MXU_N = 256
TM = 1024
FF_SPLITS = ((0, 1024), (1024, 2048), (2048, D_FF))
VMEM_LIMIT = 56 * 1024 * 1024

BF16 = jnp.bfloat16
F32 = jnp.float32


def _rmsnorm(x, g):
    return x * lax.rsqrt(jnp.mean(x * x, axis=-1, keepdims=True) + EPS) * g


def _dot(a, b):
    return jnp.dot(a, b, preferred_element_type=F32)


def _dot_t(a, b):
    return lax.dot_general(a, b, (((1,), (1,)), ((), ())), preferred_element_type=F32)


def _mem_kernel(mem_ref, g_ref, wk_ref, wv_ref, w_in_ref, w_out_ref,
                kx_ref, vx_ref, w_in_bf_ref, w_out_bf_ref):
    mn = _rmsnorm(mem_ref[...], g_ref[...]).astype(BF16)
    kx_ref[...] = (_dot(mn, wk_ref[...].astype(BF16)) * (X_HEAD_DIM ** -0.5 * LOG2E)).astype(BF16)
    vx_ref[...] = _dot(mn, wv_ref[...].astype(BF16)).astype(BF16)
    w_in_bf_ref[...] = w_in_ref[...].astype(BF16)
    w_out_bf_ref[...] = w_out_ref[...].astype(BF16)


def _mem_proj(mem, g_mem, wk, wv, w_in, w_out):
    B, M, D = mem.shape
    const = lambda b: (0, 0)
    rows = lambda b: (b, 0)
    per_mem = pl.BlockSpec((None, M, D), lambda b: (b, 0, 0))
    w_in_rows = pl.BlockSpec((w_in.shape[0] // B, w_in.shape[1]), rows)
    w_out_rows = pl.BlockSpec((w_out.shape[0] // B, w_out.shape[1]), rows)
    return pl.pallas_call(
        _mem_kernel,
        grid=(B,),
        in_specs=[
            per_mem,
            pl.BlockSpec((1, D), const),
            pl.BlockSpec((D, D), const, pipeline_mode=pl.Buffered(1)),
            pl.BlockSpec((D, D), const, pipeline_mode=pl.Buffered(1)),
            w_in_rows,
            w_out_rows,
        ],
        out_specs=[per_mem, per_mem, w_in_rows, w_out_rows],
        out_shape=[jax.ShapeDtypeStruct((B, M, D), BF16)] * 2
                  + [jax.ShapeDtypeStruct(w_in.shape, BF16), jax.ShapeDtypeStruct(w_out.shape, BF16)],
        compiler_params=pltpu.CompilerParams(
            dimension_semantics=("arbitrary",), vmem_limit_bytes=VMEM_LIMIT),
        name="mem_proj",
    )(mem, g_mem, wk, wv, w_in, w_out)


def _head_mask(e):
    lane = lax.broadcasted_iota(jnp.int32, (1, LANES), 1)
    return (lane // HEAD_DIM) == e


_BIAS_BLOCKS = ((3, 4), (0, 3, 4))


def _scores(q_s, kbuf, qb, grp):
    lane = lax.broadcasted_iota(jnp.int32, (1, LANES), 1)
    flag = jnp.broadcast_to(jnp.where(lane == 0, 1.0, 0.0).astype(BF16), (QB, LANES))
    qg = q_s[qb * QB:(qb + 1) * QB, grp * LANES:(grp + 1) * LANES]
    lhs = jnp.concatenate(
        [jnp.concatenate([jnp.where(_head_mask(e), qg, jnp.zeros_like(qg)), flag], axis=1)
         for e in range(HEADS_PER_GROUP)], axis=0)
    return _dot_t(lhs, kbuf[qb * QB:qb * QB + KWIN, grp * 2 * LANES:(grp + 1) * 2 * LANES])


def _probs(s, bias_ref, grp):
    p_rows = []
    for e in range(HEADS_PER_GROUP):
        head = grp * HEADS_PER_GROUP + e
        for i in range(QB // CHUNK):
            w0 = (i // 2) * LANES
            rows = slice(e * QB + i * CHUNK, e * QB + (i + 1) * CHUNK)
            blocks = []
            for blk in range(CWIN // LANES):
                sb = s[rows, w0 + blk * LANES:w0 + (blk + 1) * LANES]
                if blk in _BIAS_BLOCKS[i % 2]:
                    sb = sb + bias_ref[head, i % 2, :, blk * LANES:(blk + 1) * LANES]
                blocks.append(sb)
            si = jnp.concatenate(blocks, axis=1)
            p = jnp.exp2(si - jnp.max(si, axis=-1, keepdims=True)).astype(BF16)
            pad = jnp.zeros((CHUNK, KWIN - CWIN), BF16)
            p_rows.append(jnp.concatenate([p, pad] if w0 == 0 else [pad, p], axis=1))
    return jnp.concatenate(p_rows, axis=0)


def _weighted_values(p, vbuf, qb, grp):
    o = _dot(p, vbuf[qb * QB:qb * QB + KWIN, grp * 2 * LANES:(grp + 1) * 2 * LANES])
    o = o[:, 0:LANES] / o[:, LANES:2 * LANES]
    return jnp.where(_head_mask(0), o[0:QB], o[QB:2 * QB])


def _attention(q_s, kbuf, vbuf, bias_ref, abuf, between):
    order = [(qb, grp) for qb in range(TS // QB) for grp in range(N_GROUPS)]
    s_next = _scores(q_s, kbuf, *order[0])
    for j, (qb, grp) in enumerate(order):
        s_cur = s_next
        if j + 1 < len(order):
            s_next = _scores(q_s, kbuf, *order[j + 1])
        p = _probs(s_cur, bias_ref, grp)
        abuf[qb * QB:(qb + 1) * QB, grp * LANES:(grp + 1) * LANES] = (
            _weighted_values(p, vbuf, qb, grp))
        between(j)


def _mixer_kernel(x_ref, g_mix_ref, w_in_ref, bias_ref, conv_w_ref, conv_b_ref,
                  w_rg_ref, b_rg_ref, w_ig_ref, b_ig_ref, lru_l_ref,
                  g_att_ref, g_lru_ref, w_out_ref,
                  wq_ref, wo_ref, w_gate_ref, w_up_ref, w_down_ref,
                  o_ref,
                  wq_bf_ref, wo_bf_ref, w_gate_bf_ref, w_up_bf_ref, w_down_bf_ref,
                  kbuf, vbuf, q_s, ubuf, gbuf, xc_s, xcb_s, pr_s, pi_s,
                  a_s, b_s, hbuf, abuf, m_s, hcar):
    t = pl.program_id(1)

    for src, dst in ((wq_ref, wq_bf_ref), (wo_ref, wo_bf_ref), (w_gate_ref, w_gate_bf_ref),
                     (w_up_ref, w_up_bf_ref), (w_down_ref, w_down_bf_ref)):
        dst[...] = src[...].astype(BF16)

    @pl.when(t == 0)
    def _():
        lane = lax.broadcasted_iota(jnp.int32, (LEFT + TS, 2 * D_ATT), 1)
        row = lax.broadcasted_iota(jnp.int32, (LEFT + TS, 2 * D_ATT), 0)
        const_lanes = (lane // LANES) % 2 == 1
        kbuf[...] = jnp.where(const_lanes & (row < LEFT), NEG, 0.0).astype(BF16)
        vbuf[...] = jnp.where(const_lanes, 1.0, 0.0).astype(BF16)
        ubuf[0:SUBLANES, :] = jnp.zeros((SUBLANES, D_LRU), F32)
        hcar[...] = jnp.zeros_like(hcar)

    x = x_ref[...]
    h = _rmsnorm(x, g_mix_ref[...]).astype(BF16)
    q_s[...] = (_dot(h, w_in_ref[:, 0:D_ATT]) * (HEAD_DIM ** -0.5 * LOG2E)).astype(BF16)
    kf = _dot(h, w_in_ref[:, D_ATT:2 * D_ATT]).astype(BF16)
    vf = _dot(h, w_in_ref[:, 2 * D_ATT:3 * D_ATT]).astype(BF16)
    for grp in range(N_GROUPS):
        src = slice(grp * LANES, (grp + 1) * LANES)
        dst = slice(grp * 2 * LANES, grp * 2 * LANES + LANES)
        kbuf[LEFT:LEFT + TS, dst] = kf[:, src]
        vbuf[LEFT:LEFT + TS, dst] = vf[:, src]
    ubuf[SUBLANES:SUBLANES + TS, :] = _dot(h, w_in_ref[:, 3 * D_ATT:3 * D_ATT + D_LRU])
    gbuf[...] = _dot(h, w_in_ref[:, 3 * D_ATT + D_LRU:D_IN])

    def rows_of(i):
        return slice(i * SLAB, (i + 1) * SLAB)

    def conv_slab(i):
        xc = conv_b_ref[...]
        for j in range(CONV_W):
            off = i * SLAB + SUBLANES - (CONV_W - 1) + j
            xc = xc + ubuf[off:off + SLAB, :] * conv_w_ref[j:j + 1, :]
        xc_s[rows_of(i), :] = xc
        xcb_s[rows_of(i), :] = xc.astype(BF16)

    def gate_piece(c):
        gate, half = divmod(c, 2)
        cols = slice(half * MXU_N, (half + 1) * MXU_N)
        w_ref, dst = ((w_rg_ref, pr_s), (w_ig_ref, pi_s))[gate]
        dst[:, cols] = _dot(xcb_s[:, cols], w_ref[half])

    neg_l = -lru_l_ref[...]
    decay_rate = (-LRU_C * LOG2E) * (jnp.maximum(neg_l, 0.0) + jnp.log1p(jnp.exp(-jnp.abs(neg_l))))

    def coef_slab(i):
        r = pl.reciprocal(1.0 + jnp.exp2(pr_s[rows_of(i), :] + b_rg_ref[...]))
        ig = pl.reciprocal(1.0 + jnp.exp2(pi_s[rows_of(i), :] + b_ig_ref[...]))
        a = jnp.exp2(decay_rate * r)
        z = jnp.maximum(1.0 - a * a, 0.0)
        mult = jnp.where(z > 0.0, z * lax.rsqrt(z), 0.0)
        a_s[rows_of(i), :] = a
        b_s[rows_of(i), :] = mult * (ig * xc_s[rows_of(i), :])

    def scan_slab(i, carry):
        groups = SLAB // SUBLANES
        a = a_s[rows_of(i), :].reshape(groups, SUBLANES, D_LRU)
        b = b_s[rows_of(i), :].reshape(groups, SUBLANES, D_LRU)
        row = lax.broadcasted_iota(jnp.int32, (1, SUBLANES, D_LRU), 1)
        for d in (1, 2, 4):
            keep = row >= d
            b = jnp.where(keep, a * pltpu.roll(b, d, axis=1) + b, b)
            a = jnp.where(keep, a * pltpu.roll(a, d, axis=1), a)
        for g in range(groups):
            hg = a[g] * carry + b[g]
            r0 = i * SLAB + g * SUBLANES
            hbuf[r0:r0 + SUBLANES, :] = hg
            carry = hg[SUBLANES - 1:SUBLANES, :]
        return carry

    def gated_slab(i):
        rec = hbuf[rows_of(i), :] * jax.nn.gelu(gbuf[rows_of(i), :])
        m_s[rows_of(i), D_ATT:D_MODEL] = _rmsnorm(rec, g_lru_ref[...]).astype(BF16)

    n_slabs = TS // SLAB
    n_blocks = (TS // QB) * N_GROUPS
    conv_blocks = n_blocks // 4
    state = {"carry": hcar[0:1, :], "done": 0}

    def between(j):
        if j < conv_blocks:
            per = n_slabs // conv_blocks
            for i in range(j * per, (j + 1) * per):
                conv_slab(i)
            if j == conv_blocks - 1:
                for c in range(4):
                    gate_piece(c)
        else:
            upto = (j + 1 - conv_blocks) * n_slabs // (n_blocks - conv_blocks)
            for i in range(state["done"], upto):
                coef_slab(i)
                state["carry"] = scan_slab(i, state["carry"])
                gated_slab(i)
            state["done"] = upto

    _attention(q_s, kbuf, vbuf, bias_ref, abuf, between)
    hcar[0:1, :] = state["carry"]
    ubuf[0:SUBLANES, :] = ubuf[TS:TS + SUBLANES, :]
    kbuf[0:LEFT, :] = kbuf[TS:TS + LEFT, :]
    vbuf[0:LEFT, :] = vbuf[TS:TS + LEFT, :]

    m_s[:, 0:D_ATT] = _rmsnorm(abuf[...], g_att_ref[...]).astype(BF16)
    o_ref[...] = x + _dot(m_s[...], w_out_ref[...])


def _const_spec(shape):
    nd = len(shape)
    return pl.BlockSpec(shape, lambda *_: (0,) * nd, pipeline_mode=pl.Buffered(1))


def _cast_spec(shape, nt, steps):
    rows = next(r for r in range(16, shape[0] + 1, 16)
                if shape[0] % r == 0 and shape[0] // r <= steps)
    last = shape[0] // rows - 1
    return pl.BlockSpec((rows, shape[1]), lambda b, t: (jnp.minimum(b * nt + t, last), 0))


def _mixer(x, g_mix, w_in, bias_tbl, conv_w, conv_b, w_rg, b_rg, w_ig, b_ig, lru_l,
           g_att, g_lru, w_out, next_weights):
    B, S, D = x.shape
    nt = S // TS
    tile = pl.BlockSpec((None, TS, D), lambda b, t: (b, t, 0))
    args = (g_mix, w_in, bias_tbl, conv_w, conv_b, w_rg, b_rg, w_ig, b_ig, lru_l,
            g_att, g_lru, w_out)
    cast_specs = [_cast_spec(w.shape, nt, B * nt) for w in next_weights]
    kv = pltpu.VMEM((LEFT + TS, 2 * D_ATT), BF16)
    lru = pltpu.VMEM((TS, D_LRU), F32)
    return pl.pallas_call(
        _mixer_kernel,
        grid=(B, S // TS),
        in_specs=[tile] + [_const_spec(a.shape) for a in args] + cast_specs,
        out_specs=[tile] + cast_specs,
        out_shape=[jax.ShapeDtypeStruct((B, S, D), F32)]
                  + [jax.ShapeDtypeStruct(w.shape, BF16) for w in next_weights],
        scratch_shapes=[
            kv,
            kv,
            pltpu.VMEM((TS, D_ATT), BF16),
            pltpu.VMEM((SUBLANES + TS, D_LRU), F32),
            lru,
            lru,
            pltpu.VMEM((TS, D_LRU), BF16),
            lru, lru,
            lru, lru,
            lru,
            pltpu.VMEM((TS, D_ATT), F32),
            pltpu.VMEM((TS, D_MODEL), BF16),
            pltpu.VMEM((SUBLANES, D_LRU), F32),
        ],
        compiler_params=pltpu.CompilerParams(
            dimension_semantics=("arbitrary", "arbitrary"), vmem_limit_bytes=VMEM_LIMIT),
        name="mixer",
    )(x, *args, *next_weights)


def _cross_ffn_kernel(x_ref, kx_ref, vx_ref, g_cross_ref, wq_ref, wo_ref,
                      g_ffn_ref, w_gate_ref, w_up_ref, w_down_ref, g_final_ref, o_ref):
    def half(rows):
        x = x_ref[rows, :]
        hc = _rmsnorm(x, g_cross_ref[...]).astype(BF16)
        yield
        q = _dot(hc, wq_ref[...]).astype(BF16)
        yield

        def scores(hd):
            cols = slice(hd * X_HEAD_DIM, (hd + 1) * X_HEAD_DIM)
            return _dot_t(q[:, cols], kx_ref[:, cols])

        heads = []
        for hd in range(X_HEADS):
            cols = slice(hd * X_HEAD_DIM, (hd + 1) * X_HEAD_DIM)
            s = scores(hd)
            yield
            p = jnp.exp2(s - jnp.max(s, axis=-1, keepdims=True))
            l = jnp.sum(p, axis=-1, keepdims=True)
            yield
            heads.append((_dot(p.astype(BF16), vx_ref[:, cols]) / l).astype(BF16))
        x = x + _dot(jnp.concatenate(heads, axis=1), wo_ref[...])
        yield
        hf = _rmsnorm(x, g_ffn_ref[...]).astype(BF16)
        yield
        y = x
        for lo, hi in FF_SPLITS:
            gate = _dot(hf, w_gate_ref[:, lo:hi])
            up = _dot(hf, w_up_ref[:, lo:hi])
            yield
            act = (jax.nn.silu(gate) * up).astype(BF16)
            yield
            y = y + _dot(act, w_down_ref[lo:hi, :])
        yield
        o_ref[rows, :] = _rmsnorm(y, g_final_ref[...])

    n_half = 2
    rows_per = TM // n_half
    gens = [half(slice(k * rows_per, (k + 1) * rows_per)) for k in range(n_half)]
    next(gens[0])
    live = list(gens)
    while live:
        for g in list(live):
            try:
                next(g)
            except StopIteration:
                live.remove(g)


def _cross_ffn(x, kx, vx, g_cross, wq, wo, g_ffn, w_gate, w_up, w_down, g_final):
    B, S, D = x.shape
    M = kx.shape[1]
    tile = pl.BlockSpec((None, TM, D), lambda b, t: (b, t, 0))
    memspec = pl.BlockSpec((None, M, D), lambda b, t: (b, 0, 0))
    args = (g_cross, wq, wo, g_ffn, w_gate, w_up, w_down, g_final)
    return pl.pallas_call(
        _cross_ffn_kernel,
        grid=(B, S // TM),
        in_specs=[tile, memspec, memspec] + [_const_spec(a.shape) for a in args],
        out_specs=tile,
        out_shape=jax.ShapeDtypeStruct((B, S, D), F32),
        compiler_params=pltpu.CompilerParams(
            dimension_semantics=("arbitrary", "arbitrary"), vmem_limit_bytes=VMEM_LIMIT),
        name="cross_ffn",
    )(x, kx, vx, *args)


def _bias_table(rel_bias):
    n_heads = rel_bias.shape[0]
    rb = rel_bias.astype(F32) * LOG2E
    slot = np.arange(CWIN) // CHUNK
    tbl = []
    for parity in range(2):
        rel_max = (parity + LEFT_CHUNKS) * CHUNK + (CHUNK - 1)
        rel_min = (parity + LEFT_CHUNKS) * CHUNK - (CWIN - 1)
        hi, lo = min(rel_max, MAX_REL), max(rel_min, -MAX_REL)
        vec = jnp.concatenate([
            jnp.broadcast_to(rb[:, 2 * MAX_REL:], (n_heads, rel_max - hi)),
            jnp.flip(rb[:, lo + MAX_REL:hi + MAX_REL + 1], axis=1),
            jnp.broadcast_to(rb[:, :1], (n_heads, lo - rel_min)),
        ], axis=1)
        rows = jnp.stack([vec[:, CHUNK - 1 - qi:CHUNK - 1 - qi + CWIN] for qi in range(CHUNK)],
                         axis=1)
        band = (slot >= parity) & (slot <= parity + LEFT_CHUNKS)
        far = rb[:, 2 * MAX_REL:, None]
        tbl.append(jnp.where(band[None, None, :], rows - far, NEG))
    return jnp.stack(tbl, axis=1)


def _block_diag(w):
    per = LRU_BLOCKS // 2
    halves = []
    for hlf in range(2):
        rows = []
        for i in range(per):
            blocks = [w[hlf * per + i] if j == i else jnp.zeros((LRU_BLOCK, LRU_BLOCK), w.dtype)
                      for j in range(per)]
            rows.append(jnp.concatenate(blocks, axis=1))
        halves.append(jnp.concatenate(rows, axis=0))
    return jnp.stack(halves).astype(BF16)


def kernel(x, mem, g_mix, w_in, rel_bias, conv_w, conv_b, w_rg, b_rg, w_ig, b_ig, lru_L,
           g_out_attn, g_out_lru, w_out, g_cross, g_mem, wq_c, wk_c, wv_c, wo_c,
           g_ffn, w_gate, w_up, w_down, g_final):
    depth = g_mix.shape[0]
    row = lambda v: v.reshape(1, -1)
    for l in range(depth):
        kx, vx, w_in_bf, w_out_bf = _mem_proj(mem, row(g_mem[l]), wk_c[l], wv_c[l], w_in[l], w_out[l])
        x, wq_bf, wo_bf, w_gate_bf, w_up_bf, w_down_bf = _mixer(
            x, row(g_mix[l]), w_in_bf, _bias_table(rel_bias[l]),
            conv_w[l], row(conv_b[l]), _block_diag(-LOG2E * w_rg[l]), row(-LOG2E * b_rg[l]),
            _block_diag(-LOG2E * w_ig[l]), row(-LOG2E * b_ig[l]), row(lru_L[l]),
            row(g_out_attn[l]), row(g_out_lru[l]), w_out_bf,
            (wq_c[l], wo_c[l], w_gate[l], w_up[l], w_down[l]))
        assert depth == 1
        x = _cross_ffn(x, kx, vx, row(g_cross[l]), wq_bf, wo_bf, row(g_ffn[l]),
                       w_gate_bf, w_up_bf, w_down_bf, row(g_final))
    return x
```

```python
import functools

import numpy as np
import jax
import jax.numpy as jnp
from jax import lax
from jax.experimental import pallas as pl
from jax.experimental.pallas import tpu as pltpu

D_MODEL = 1024
CHUNK = 64
N_MEM = 256
ATT_HEADS = 8
HEAD_DIM = 64
D_ATT = ATT_HEADS * HEAD_DIM
D_LRU = D_MODEL - D_ATT
LRU_BLOCKS = 8
LRU_BLOCK = D_LRU // LRU_BLOCKS
CONV_W = 4
LRU_C = 8.0
LEFT_CHUNKS = 8
MAX_REL = 128
X_HEADS = 4
X_HEAD_DIM = D_MODEL // X_HEADS
D_FF = 2816
D_IN = 3 * D_ATT + 2 * D_LRU
EPS = 1e-6
NEG = -1e30
LOG2E = 1.4426950408889634

LANES = 128
SUBLANES = 8
LEFT = LEFT_CHUNKS * CHUNK
QB = 4 * CHUNK
KWIN = QB + LEFT
CWIN = 10 * CHUNK
HEADS_PER_GROUP = LANES // HEAD_DIM
N_GROUPS = D_ATT // LANES
TS = 1024
SLAB = 128
MXU_N = 256
TM = 1024
FF_SPLITS = ((0, 1024), (1024, 2048), (2048, D_FF))
VMEM_CAPACITY = 64 * 1024 * 1024
VMEM_LIMIT = VMEM_CAPACITY - 4 * 1024 * 1024

BF16 = jnp.bfloat16
F32 = jnp.float32


def _rmsnorm(x, g):
    return x * lax.rsqrt(jnp.mean(x * x, axis=-1, keepdims=True) + EPS) * g


def _dot(a, b):
    return jnp.dot(a, b, preferred_element_type=F32)


def _dot_t(a, b):
    return lax.dot_general(a, b, (((1,), (1,)), ((), ())), preferred_element_type=F32)


def _mem_kernel(mem_ref, g_ref, wk_ref, wv_ref, w_in_ref, w_out_ref,
                kx_ref, vx_ref, w_in_bf_ref, w_out_bf_ref):
    mn = _rmsnorm(mem_ref[...], g_ref[...]).astype(BF16)
    kx_ref[...] = (_dot(mn, wk_ref[...].astype(BF16)) * (X_HEAD_DIM ** -0.5 * LOG2E)).astype(BF16)
    vx_ref[...] = _dot(mn, wv_ref[...].astype(BF16)).astype(BF16)
    w_in_bf_ref[...] = w_in_ref[...].astype(BF16)
    w_out_bf_ref[...] = w_out_ref[...].astype(BF16)


def _mem_proj(mem, g_mem, wk, wv, w_in, w_out):
    B, M, D = mem.shape
    const = lambda b: (0, 0)
    rows = lambda b: (b, 0)
    per_mem = pl.BlockSpec((None, M, D), lambda b: (b, 0, 0))
    w_in_rows = pl.BlockSpec((w_in.shape[0] // B, w_in.shape[1]), rows)
    w_out_rows = pl.BlockSpec((w_out.shape[0] // B, w_out.shape[1]), rows)
    return pl.pallas_call(
        _mem_kernel,
        grid=(B,),
        in_specs=[
            per_mem,
            pl.BlockSpec((1, D), const),
            pl.BlockSpec((D, D), const, pipeline_mode=pl.Buffered(1)),
            pl.BlockSpec((D, D), const, pipeline_mode=pl.Buffered(1)),
            w_in_rows,
            w_out_rows,
        ],
        out_specs=[per_mem, per_mem, w_in_rows, w_out_rows],
        out_shape=[jax.ShapeDtypeStruct((B, M, D), BF16)] * 2
                  + [jax.ShapeDtypeStruct(w_in.shape, BF16), jax.ShapeDtypeStruct(w_out.shape, BF16)],
        compiler_params=pltpu.CompilerParams(
            dimension_semantics=("arbitrary",), vmem_limit_bytes=VMEM_LIMIT),
        name="mem_proj",
    )(mem, g_mem, wk, wv, w_in, w_out)


def _head_mask(e):
    lane = lax.broadcasted_iota(jnp.int32, (1, LANES), 1)
    return (lane // HEAD_DIM) == e


_BIAS_BLOCKS = ((3, 4), (0, 3, 4))


def _scores(q_s, kbuf, qb, grp):
    lane = lax.broadcasted_iota(jnp.int32, (1, LANES), 1)
    flag = jnp.broadcast_to(jnp.where(lane == 0, 1.0, 0.0).astype(BF16), (QB, LANES))
    qg = q_s[qb * QB:(qb + 1) * QB, grp * LANES:(grp + 1) * LANES]
    lhs = jnp.concatenate(
        [jnp.concatenate([jnp.where(_head_mask(e), qg, jnp.zeros_like(qg)), flag], axis=1)
         for e in range(HEADS_PER_GROUP)], axis=0)
    return _dot_t(lhs, kbuf[qb * QB:qb * QB + KWIN, grp * 2 * LANES:(grp + 1) * 2 * LANES])


def _probs(s, bias_ref, grp):
    p_rows = []
    for e in range(HEADS_PER_GROUP):
        head = grp * HEADS_PER_GROUP + e
        for i in range(QB // CHUNK):
            w0 = (i // 2) * LANES
            rows = slice(e * QB + i * CHUNK, e * QB + (i + 1) * CHUNK)
            blocks = []
            for blk in range(CWIN // LANES):
                sb = s[rows, w0 + blk * LANES:w0 + (blk + 1) * LANES]
                if blk in _BIAS_BLOCKS[i % 2]:
                    sb = sb + bias_ref[head, i % 2, :, blk * LANES:(blk + 1) * LANES]
                blocks.append(sb)
            si = jnp.concatenate(blocks, axis=1)
            p = jnp.exp2(si - jnp.max(si, axis=-1, keepdims=True)).astype(BF16)
            pad = jnp.zeros((CHUNK, KWIN - CWIN), BF16)
            p_rows.append(jnp.concatenate([p, pad] if w0 == 0 else [pad, p], axis=1))
    return jnp.concatenate(p_rows, axis=0)


def _weighted_values(p, vbuf, qb, grp):
    o = _dot(p, vbuf[qb * QB:qb * QB + KWIN, grp * 2 * LANES:(grp + 1) * 2 * LANES])
    o = o[:, 0:LANES] / o[:, LANES:2 * LANES]
    return jnp.where(_head_mask(0), o[0:QB], o[QB:2 * QB])


def _attention(q_s, kbuf, vbuf, bias_ref, abuf, between):
    order = [(qb, grp) for qb in range(TS // QB) for grp in range(N_GROUPS)]
    s_next = _scores(q_s, kbuf, *order[0])
    for j, (qb, grp) in enumerate(order):
        s_cur = s_next
        if j + 1 < len(order):
            s_next = _scores(q_s, kbuf, *order[j + 1])
        p = _probs(s_cur, bias_ref, grp)
        abuf[qb * QB:(qb + 1) * QB, grp * LANES:(grp + 1) * LANES] = (
            _weighted_values(p, vbuf, qb, grp))
        between(j)


def _mixer_kernel(x_ref, g_mix_ref, w_in_ref, bias_ref, conv_w_ref, conv_b_ref,
                  w_rg_ref, b_rg_ref, w_ig_ref, b_ig_ref, lru_l_ref,
                  g_att_ref, g_lru_ref, w_out_ref,
                  wq_ref, wo_ref, w_gate_ref, w_up_ref, w_down_ref,
                  o_ref,
                  wq_bf_ref, wo_bf_ref, w_gate_bf_ref, w_up_bf_ref, w_down_bf_ref,
                  kbuf, vbuf, q_s, ubuf, gbuf, xc_s, xcb_s, pr_s, pi_s,
                  a_s, b_s, hbuf, abuf, m_s, hcar):
    t = pl.program_id(1)

    for src, dst in ((wq_ref, wq_bf_ref), (wo_ref, wo_bf_ref), (w_gate_ref, w_gate_bf_ref),
                     (w_up_ref, w_up_bf_ref), (w_down_ref, w_down_bf_ref)):
        dst[...] = src[...].astype(BF16)

    @pl.when(t == 0)
    def _():
        lane = lax.broadcasted_iota(jnp.int32, (LEFT + TS, 2 * D_ATT), 1)
        row = lax.broadcasted_iota(jnp.int32, (LEFT + TS, 2 * D_ATT), 0)
        const_lanes = (lane // LANES) % 2 == 1
        kbuf[...] = jnp.where(const_lanes & (row < LEFT), NEG, 0.0).astype(BF16)
        vbuf[...] = jnp.where(const_lanes, 1.0, 0.0).astype(BF16)
        ubuf[0:SUBLANES, :] = jnp.zeros((SUBLANES, D_LRU), F32)
        hcar[...] = jnp.zeros_like(hcar)

    x = x_ref[...]
    h = _rmsnorm(x, g_mix_ref[...]).astype(BF16)
    q_s[...] = (_dot(h, w_in_ref[:, 0:D_ATT]) * (HEAD_DIM ** -0.5 * LOG2E)).astype(BF16)
    kf = _dot(h, w_in_ref[:, D_ATT:2 * D_ATT]).astype(BF16)
    vf = _dot(h, w_in_ref[:, 2 * D_ATT:3 * D_ATT]).astype(BF16)
    for grp in range(N_GROUPS):
        src = slice(grp * LANES, (grp + 1) * LANES)
        dst = slice(grp * 2 * LANES, grp * 2 * LANES + LANES)
        kbuf[LEFT:LEFT + TS, dst] = kf[:, src]
        vbuf[LEFT:LEFT + TS, dst] = vf[:, src]
    ubuf[SUBLANES:SUBLANES + TS, :] = _dot(h, w_in_ref[:, 3 * D_ATT:3 * D_ATT + D_LRU])
    gbuf[...] = _dot(h, w_in_ref[:, 3 * D_ATT + D_LRU:D_IN])

    def rows_of(i):
        return slice(i * SLAB, (i + 1) * SLAB)

    def conv_slab(i):
        xc = conv_b_ref[...]
        for j in range(CONV_W):
            off = i * SLAB + SUBLANES - (CONV_W - 1) + j
            xc = xc + ubuf[off:off + SLAB, :] * conv_w_ref[j:j + 1, :]
        xc_s[rows_of(i), :] = xc
        xcb_s[rows_of(i), :] = xc.astype(BF16)

    def gate_piece(c):
        gate, half = divmod(c, 2)
        cols = slice(half * MXU_N, (half + 1) * MXU_N)
        w_ref, dst = ((w_rg_ref, pr_s), (w_ig_ref, pi_s))[gate]
        dst[:, cols] = _dot(xcb_s[:, cols], w_ref[half])

    neg_l = -lru_l_ref[...]
    decay_rate = (-LRU_C * LOG2E) * (jnp.maximum(neg_l, 0.0) + jnp.log1p(jnp.exp(-jnp.abs(neg_l))))

    def coef_slab(i):
        r = pl.reciprocal(1.0 + jnp.exp2(pr_s[rows_of(i), :] + b_rg_ref[...]))
        ig = pl.reciprocal(1.0 + jnp.exp2(pi_s[rows_of(i), :] + b_ig_ref[...]))
        a = jnp.exp2(decay_rate * r)
        z = jnp.maximum(1.0 - a * a, 0.0)
        mult = jnp.where(z > 0.0, z * lax.rsqrt(z), 0.0)
        a_s[rows_of(i), :] = a
        b_s[rows_of(i), :] = mult * (ig * xc_s[rows_of(i), :])

    def scan_slab(i, carry):
        groups = SLAB // SUBLANES
        a = a_s[rows_of(i), :].reshape(groups, SUBLANES, D_LRU)
        b = b_s[rows_of(i), :].reshape(groups, SUBLANES, D_LRU)
        row = lax.broadcasted_iota(jnp.int32, (1, SUBLANES, D_LRU), 1)
        for d in (1, 2, 4):
            keep = row >= d
            b = jnp.where(keep, a * pltpu.roll(b, d, axis=1) + b, b)
            a = jnp.where(keep, a * pltpu.roll(a, d, axis=1), a)
        for g in range(groups):
            hg = a[g] * carry + b[g]
            r0 = i * SLAB + g * SUBLANES
            hbuf[r0:r0 + SUBLANES, :] = hg
            carry = hg[SUBLANES - 1:SUBLANES, :]
        return carry

    def gated_slab(i):
        rec = hbuf[rows_of(i), :] * jax.nn.gelu(gbuf[rows_of(i), :])
        m_s[rows_of(i), D_ATT:D_MODEL] = _rmsnorm(rec, g_lru_ref[...]).astype(BF16)

    n_slabs = TS // SLAB
    n_blocks = (TS // QB) * N_GROUPS
    assert n_slabs < n_blocks
    state = {"carry": hcar[0:1, :]}

    def between(j):
        if j == 0:
            for i in range(n_slabs):
                conv_slab(i)
            for c in range(4):
                gate_piece(c)
        elif j <= n_slabs:
            i = j - 1
            coef_slab(i)
            state["carry"] = scan_slab(i, state["carry"])
            gated_slab(i)

    _attention(q_s, kbuf, vbuf, bias_ref, abuf, between)
    hcar[0:1, :] = state["carry"]
    ubuf[0:SUBLANES, :] = ubuf[TS:TS + SUBLANES, :]
    kbuf[0:LEFT, :] = kbuf[TS:TS + LEFT, :]
    vbuf[0:LEFT, :] = vbuf[TS:TS + LEFT, :]

    m_s[:, 0:D_ATT] = _rmsnorm(abuf[...], g_att_ref[...]).astype(BF16)
    o_ref[...] = x + _dot(m_s[...], w_out_ref[...])


def _const_spec(shape):
    nd = len(shape)
    return pl.BlockSpec(shape, lambda *_: (0,) * nd, pipeline_mode=pl.Buffered(1))


def _cast_spec(shape, nt, steps):
    rows = next(r for r in range(16, shape[0] + 1, 16)
                if shape[0] % r == 0 and shape[0] // r <= steps)
    last = shape[0] // rows - 1
    return pl.BlockSpec((rows, shape[1]), lambda b, t: (jnp.minimum(b * nt + t, last), 0))


def _mixer(x, g_mix, w_in, bias_tbl, conv_w, conv_b, w_rg, b_rg, w_ig, b_ig, lru_l,
           g_att, g_lru, w_out, next_weights):
    B, S, D = x.shape
    nt = S // TS
    tile = pl.BlockSpec((None, TS, D), lambda b, t: (b, t, 0))
    args = (g_mix, w_in, bias_tbl, conv_w, conv_b, w_rg, b_rg, w_ig, b_ig, lru_l,
            g_att, g_lru, w_out)
    cast_specs = [_cast_spec(w.shape, nt, B * nt) for w in next_weights]
    kv = pltpu.VMEM((LEFT + TS, 2 * D_ATT), BF16)
    lru = pltpu.VMEM((TS, D_LRU), F32)
    return pl.pallas_call(
        _mixer_kernel,
        grid=(B, S // TS),
        in_specs=[tile] + [_const_spec(a.shape) for a in args] + cast_specs,
        out_specs=[tile] + cast_specs,
        out_shape=[jax.ShapeDtypeStruct((B, S, D), F32)]
                  + [jax.ShapeDtypeStruct(w.shape, BF16) for w in next_weights],
        scratch_shapes=[
            kv,
            kv,
            pltpu.VMEM((TS, D_ATT), BF16),
            pltpu.VMEM((SUBLANES + TS, D_LRU), F32),
            lru,
            lru,
            pltpu.VMEM((TS, D_LRU), BF16),
            lru, lru,
            lru, lru,
            lru,
            pltpu.VMEM((TS, D_ATT), F32),
            pltpu.VMEM((TS, D_MODEL), BF16),
            pltpu.VMEM((SUBLANES, D_LRU), F32),
        ],
        compiler_params=pltpu.CompilerParams(
            dimension_semantics=("arbitrary", "arbitrary"), vmem_limit_bytes=VMEM_LIMIT),
        name="mixer",
    )(x, *args, *next_weights)


def _cross_ffn_kernel(x_ref, kx_ref, vx_ref, g_cross_ref, wq_ref, wo_ref,
                      g_ffn_ref, w_gate_ref, w_up_ref, w_down_ref, g_final_ref, o_ref):
    def half(rows):
        x = x_ref[rows, :]
        hc = _rmsnorm(x, g_cross_ref[...]).astype(BF16)
        yield
        q = _dot(hc, wq_ref[...]).astype(BF16)
        yield

        def scores(hd):
            cols = slice(hd * X_HEAD_DIM, (hd + 1) * X_HEAD_DIM)
            return _dot_t(q[:, cols], kx_ref[:, cols])

        heads = []
        for hd in range(X_HEADS):
            cols = slice(hd * X_HEAD_DIM, (hd + 1) * X_HEAD_DIM)
            s = scores(hd)
            yield
            p = jnp.exp2(s - jnp.max(s, axis=-1, keepdims=True))
            l = jnp.sum(p, axis=-1, keepdims=True)
            yield
            heads.append((_dot(p.astype(BF16), vx_ref[:, cols]) / l).astype(BF16))
        x = x + _dot(jnp.concatenate(heads, axis=1), wo_ref[...])
        yield
        hf = _rmsnorm(x, g_ffn_ref[...]).astype(BF16)
        yield
        y = x
        for lo, hi in FF_SPLITS:
            gate = _dot(hf, w_gate_ref[:, lo:hi])
            up = _dot(hf, w_up_ref[:, lo:hi])
            yield
            act = (jax.nn.silu(gate) * up).astype(BF16)
            yield
            y = y + _dot(act, w_down_ref[lo:hi, :])
        yield
        o_ref[rows, :] = _rmsnorm(y, g_final_ref[...])

    n_half = 2
    rows_per = TM // n_half
    gens = [half(slice(k * rows_per, (k + 1) * rows_per)) for k in range(n_half)]
    next(gens[0])
    live = list(gens)
    while live:
        for g in list(live):
            try:
                next(g)
            except StopIteration:
                live.remove(g)


def _cross_ffn(x, kx, vx, g_cross, wq, wo, g_ffn, w_gate, w_up, w_down, g_final):
    B, S, D = x.shape
    M = kx.shape[1]
    tile = pl.BlockSpec((None, TM, D), lambda b, t: (b, t, 0))
    memspec = pl.BlockSpec((None, M, D), lambda b, t: (b, 0, 0))
    args = (g_cross, wq, wo, g_ffn, w_gate, w_up, w_down, g_final)
    return pl.pallas_call(
        _cross_ffn_kernel,
        grid=(B, S // TM),
        in_specs=[tile, memspec, memspec] + [_const_spec(a.shape) for a in args],
        out_specs=tile,
        out_shape=jax.ShapeDtypeStruct((B, S, D), F32),
        compiler_params=pltpu.CompilerParams(
            dimension_semantics=("arbitrary", "arbitrary"), vmem_limit_bytes=VMEM_LIMIT),
        name="cross_ffn",
    )(x, kx, vx, *args)


def _bias_table(rel_bias):
    n_heads = rel_bias.shape[0]
    rb = rel_bias.astype(F32) * LOG2E
    slot = np.arange(CWIN) // CHUNK
    tbl = []
    for parity in range(2):
        rel_max = (parity + LEFT_CHUNKS) * CHUNK + (CHUNK - 1)
        rel_min = (parity + LEFT_CHUNKS) * CHUNK - (CWIN - 1)
        hi, lo = min(rel_max, MAX_REL), max(rel_min, -MAX_REL)
        vec = jnp.concatenate([
            jnp.broadcast_to(rb[:, 2 * MAX_REL:], (n_heads, rel_max - hi)),
            jnp.flip(rb[:, lo + MAX_REL:hi + MAX_REL + 1], axis=1),
            jnp.broadcast_to(rb[:, :1], (n_heads, lo - rel_min)),
        ], axis=1)
        rows = jnp.stack([vec[:, CHUNK - 1 - qi:CHUNK - 1 - qi + CWIN] for qi in range(CHUNK)],
                         axis=1)
        band = (slot >= parity) & (slot <= parity + LEFT_CHUNKS)
        far = rb[:, 2 * MAX_REL:, None]
        tbl.append(jnp.where(band[None, None, :], rows - far, NEG))
    return jnp.stack(tbl, axis=1)


def _block_diag(w):
    per = LRU_BLOCKS // 2
    halves = []
    for hlf in range(2):
        rows = []
        for i in range(per):
            blocks = [w[hlf * per + i] if j == i else jnp.zeros((LRU_BLOCK, LRU_BLOCK), w.dtype)
                      for j in range(per)]
            rows.append(jnp.concatenate(blocks, axis=1))
        halves.append(jnp.concatenate(rows, axis=0))
    return jnp.stack(halves).astype(BF16)


def kernel(x, mem, g_mix, w_in, rel_bias, conv_w, conv_b, w_rg, b_rg, w_ig, b_ig, lru_L,
           g_out_attn, g_out_lru, w_out, g_cross, g_mem, wq_c, wk_c, wv_c, wo_c,
           g_ffn, w_gate, w_up, w_down, g_final):
    depth = g_mix.shape[0]
    row = lambda v: v.reshape(1, -1)
    for l in range(depth):
        kx, vx, w_in_bf, w_out_bf = _mem_proj(mem, row(g_mem[l]), wk_c[l], wv_c[l], w_in[l], w_out[l])
        x, wq_bf, wo_bf, w_gate_bf, w_up_bf, w_down_bf = _mixer(
            x, row(g_mix[l]), w_in_bf, _bias_table(rel_bias[l]),
            conv_w[l], row(conv_b[l]), _block_diag(-LOG2E * w_rg[l]), row(-LOG2E * b_rg[l]),
            _block_diag(-LOG2E * w_ig[l]), row(-LOG2E * b_ig[l]), row(lru_L[l]),
            row(g_out_attn[l]), row(g_out_lru[l]), w_out_bf,
            (wq_c[l], wo_c[l], w_gate[l], w_up[l], w_down[l]))
        assert depth == 1
        x = _cross_ffn(x, kx, vx, row(g_cross[l]), wq_bf, wo_bf, row(g_ffn[l]),
                       w_gate_bf, w_up_bf, w_down_bf, row(g_final))
    return x
```

```python
import functools

import numpy as np
import jax
import jax.numpy as jnp
from jax import lax
from jax.experimental import pallas as pl
from jax.experimental.pallas import tpu as pltpu

D_MODEL = 1024
CHUNK = 64
N_MEM = 256
ATT_HEADS = 8
HEAD_DIM = 64
D_ATT = ATT_HEADS * HEAD_DIM
D_LRU = D_MODEL - D_ATT
LRU_BLOCKS = 8
LRU_BLOCK = D_LRU // LRU_BLOCKS
CONV_W = 4
LRU_C = 8.0
LEFT_CHUNKS = 8
MAX_REL = 128
X_HEADS = 4
X_HEAD_DIM = D_MODEL // X_HEADS
D_FF = 2816
D_IN = 3 * D_ATT + 2 * D_LRU
EPS = 1e-6
NEG = -1e30
LOG2E = 1.4426950408889634

LANES = 128
SUBLANES = 8
LEFT = LEFT_CHUNKS * CHUNK
QB = 4 * CHUNK
KWIN = QB + LEFT
CWIN = 10 * CHUNK
HEADS_PER_GROUP = LANES // HEAD_DIM
N_GROUPS = D_ATT // LANES
TS = 512
SLAB = 128
MXU_N = 256
TM = 1024
FF_SPLITS = ((0, 1024), (1024, 2048), (2048, D_FF))
VMEM_CAPACITY = 64 * 1024 * 1024
VMEM_LIMIT = VMEM_CAPACITY - 4 * 1024 * 1024

BF16 = jnp.bfloat16
F32 = jnp.float32


def _rmsnorm(x, g):
    return x * lax.rsqrt(jnp.mean(x * x, axis=-1, keepdims=True) + EPS) * g


def _dot(a, b):
    return jnp.dot(a, b, preferred_element_type=F32)


def _dot_t(a, b):
    return lax.dot_general(a, b, (((1,), (1,)), ((), ())), preferred_element_type=F32)


def _mem_kernel(mem_ref, g_ref, wk_ref, wv_ref, w_in_ref, w_out_ref,
                kx_ref, vx_ref, w_in_bf_ref, w_out_bf_ref):
    mn = _rmsnorm(mem_ref[...], g_ref[...]).astype(BF16)
    kx_ref[...] = (_dot(mn, wk_ref[...].astype(BF16)) * (X_HEAD_DIM ** -0.5 * LOG2E)).astype(BF16)
    vx_ref[...] = _dot(mn, wv_ref[...].astype(BF16)).astype(BF16)
    w_in_bf_ref[...] = w_in_ref[...].astype(BF16)
    w_out_bf_ref[...] = w_out_ref[...].astype(BF16)


def _mem_proj(mem, g_mem, wk, wv, w_in, w_out):
    B, M, D = mem.shape
    const = lambda b: (0, 0)
    rows = lambda b: (b, 0)
    per_mem = pl.BlockSpec((None, M, D), lambda b: (b, 0, 0))
    w_in_rows = pl.BlockSpec((w_in.shape[0] // B, w_in.shape[1]), rows)
    w_out_rows = pl.BlockSpec((w_out.shape[0] // B, w_out.shape[1]), rows)
    return pl.pallas_call(
        _mem_kernel,
        grid=(B,),
        in_specs=[
            per_mem,
            pl.BlockSpec((1, D), const),
            pl.BlockSpec((D, D), const, pipeline_mode=pl.Buffered(1)),
            pl.BlockSpec((D, D), const, pipeline_mode=pl.Buffered(1)),
            w_in_rows,
            w_out_rows,
        ],
        out_specs=[per_mem, per_mem, w_in_rows, w_out_rows],
        out_shape=[jax.ShapeDtypeStruct((B, M, D), BF16)] * 2
                  + [jax.ShapeDtypeStruct(w_in.shape, BF16), jax.ShapeDtypeStruct(w_out.shape, BF16)],
        compiler_params=pltpu.CompilerParams(
            dimension_semantics=("arbitrary",), vmem_limit_bytes=VMEM_LIMIT),
        name="mem_proj",
    )(mem, g_mem, wk, wv, w_in, w_out)


def _head_mask(e):
    lane = lax.broadcasted_iota(jnp.int32, (1, LANES), 1)
    return (lane // HEAD_DIM) == e


_BIAS_BLOCKS = ((3, 4), (0, 3, 4))


def _scores(q_s, kbuf, qb, grp):
    lane = lax.broadcasted_iota(jnp.int32, (1, LANES), 1)
    flag = jnp.broadcast_to(jnp.where(lane == 0, 1.0, 0.0).astype(BF16), (QB, LANES))
    qg = q_s[qb * QB:(qb + 1) * QB, grp * LANES:(grp + 1) * LANES]
    lhs = jnp.concatenate(
        [jnp.concatenate([jnp.where(_head_mask(e), qg, jnp.zeros_like(qg)), flag], axis=1)
         for e in range(HEADS_PER_GROUP)], axis=0)
    return _dot_t(lhs, kbuf[qb * QB:qb * QB + KWIN, grp * 2 * LANES:(grp + 1) * 2 * LANES])


def _probs(s, bias_ref, grp):
    p_rows = []
    for e in range(HEADS_PER_GROUP):
        head = grp * HEADS_PER_GROUP + e
        for i in range(QB // CHUNK):
            w0 = (i // 2) * LANES
            rows = slice(e * QB + i * CHUNK, e * QB + (i + 1) * CHUNK)
            blocks = []
            for blk in range(CWIN // LANES):
                sb = s[rows, w0 + blk * LANES:w0 + (blk + 1) * LANES]
                if blk in _BIAS_BLOCKS[i % 2]:
                    sb = sb + bias_ref[head, i % 2, :, blk * LANES:(blk + 1) * LANES]
                blocks.append(sb)
            si = jnp.concatenate(blocks, axis=1)
            p = jnp.exp2(si - jnp.max(si, axis=-1, keepdims=True)).astype(BF16)
            pad = jnp.zeros((CHUNK, KWIN - CWIN), BF16)
            p_rows.append(jnp.concatenate([p, pad] if w0 == 0 else [pad, p], axis=1))
    return jnp.concatenate(p_rows, axis=0)


def _weighted_values(p, vbuf, qb, grp):
    o = _dot(p, vbuf[qb * QB:qb * QB + KWIN, grp * 2 * LANES:(grp + 1) * 2 * LANES])
    o = o[:, 0:LANES] / o[:, LANES:2 * LANES]
    return jnp.where(_head_mask(0), o[0:QB], o[QB:2 * QB])


def _attention(q_s, kbuf, vbuf, bias_ref, abuf, between):
    order = [(qb, grp) for qb in range(TS // QB) for grp in range(N_GROUPS)]
    s_next = _scores(q_s, kbuf, *order[0])
    for j, (qb, grp) in enumerate(order):
        s_cur = s_next
        if j + 1 < len(order):
            s_next = _scores(q_s, kbuf, *order[j + 1])
        p = _probs(s_cur, bias_ref, grp)
        abuf[qb * QB:(qb + 1) * QB, grp * LANES:(grp + 1) * LANES] = (
            _weighted_values(p, vbuf, qb, grp))
        between(j)


def _mixer_kernel(x_ref, g_mix_ref, w_in_ref, bias_ref, conv_w_ref, conv_b_ref,
                  w_rg_ref, b_rg_ref, w_ig_ref, b_ig_ref, lru_l_ref,
                  g_att_ref, g_lru_ref, w_out_ref,
                  wq_ref, wo_ref, w_gate_ref, w_up_ref, w_down_ref,
                  o_ref,
                  wq_bf_ref, wo_bf_ref, w_gate_bf_ref, w_up_bf_ref, w_down_bf_ref,
                  kbuf, vbuf, q_s, ubuf, gbuf, xc_s, xcb_s, pr_s, pi_s,
                  a_s, b_s, hbuf, abuf, m_s, hcar):
    t = pl.program_id(1)

    for src, dst in ((wq_ref, wq_bf_ref), (wo_ref, wo_bf_ref), (w_gate_ref, w_gate_bf_ref),
                     (w_up_ref, w_up_bf_ref), (w_down_ref, w_down_bf_ref)):
        dst[...] = src[...].astype(BF16)

    @pl.when(t == 0)
    def _():
        lane = lax.broadcasted_iota(jnp.int32, (LEFT + TS, 2 * D_ATT), 1)
        row = lax.broadcasted_iota(jnp.int32, (LEFT + TS, 2 * D_ATT), 0)
        const_lanes = (lane // LANES) % 2 == 1
        kbuf[...] = jnp.where(const_lanes & (row < LEFT), NEG, 0.0).astype(BF16)
        vbuf[...] = jnp.where(const_lanes, 1.0, 0.0).astype(BF16)
        ubuf[0:SUBLANES, :] = jnp.zeros((SUBLANES, D_LRU), F32)
        hcar[...] = jnp.zeros_like(hcar)

    x = x_ref[...]
    h = _rmsnorm(x, g_mix_ref[...]).astype(BF16)
    ubuf[SUBLANES:SUBLANES + TS, :] = _dot(h, w_in_ref[:, 3 * D_ATT:3 * D_ATT + D_LRU])
    gbuf[...] = _dot(h, w_in_ref[:, 3 * D_ATT + D_LRU:D_IN])

    def proj_q():
        q_s[...] = (_dot(h, w_in_ref[:, 0:D_ATT]) * (HEAD_DIM ** -0.5 * LOG2E)).astype(BF16)

    def proj_kv(buf, lo):
        f = _dot(h, w_in_ref[:, lo:lo + D_ATT]).astype(BF16)
        for grp in range(N_GROUPS):
            src = slice(grp * LANES, (grp + 1) * LANES)
            dst = slice(grp * 2 * LANES, grp * 2 * LANES + LANES)
            buf[LEFT:LEFT + TS, dst] = f[:, src]

    def rows_of(i):
        return slice(i * SLAB, (i + 1) * SLAB)

    def conv_slab(i):
        xc = conv_b_ref[...]
        for j in range(CONV_W):
            off = i * SLAB + SUBLANES - (CONV_W - 1) + j
            xc = xc + ubuf[off:off + SLAB, :] * conv_w_ref[j:j + 1, :]
        xc_s[rows_of(i), :] = xc
        xcb_s[rows_of(i), :] = xc.astype(BF16)

    def gate_piece(c):
        gate, half = divmod(c, 2)
        cols = slice(half * MXU_N, (half + 1) * MXU_N)
        w_ref, dst = ((w_rg_ref, pr_s), (w_ig_ref, pi_s))[gate]
        dst[:, cols] = _dot(xcb_s[:, cols], w_ref[half])

    neg_l = -lru_l_ref[...]
    decay_rate = (-LRU_C * LOG2E) * (jnp.maximum(neg_l, 0.0) + jnp.log1p(jnp.exp(-jnp.abs(neg_l))))

    def coef_slab(i):
        r = pl.reciprocal(1.0 + jnp.exp2(pr_s[rows_of(i), :] + b_rg_ref[...]))
        ig = pl.reciprocal(1.0 + jnp.exp2(pi_s[rows_of(i), :] + b_ig_ref[...]))
        a = jnp.exp2(decay_rate * r)
        z = jnp.maximum(1.0 - a * a, 0.0)
        mult = jnp.where(z > 0.0, z * lax.rsqrt(z), 0.0)
        a_s[rows_of(i), :] = a
        b_s[rows_of(i), :] = mult * (ig * xc_s[rows_of(i), :])

    def scan_slab(i, carry):
        groups = SLAB // SUBLANES
        a = a_s[rows_of(i), :].reshape(groups, SUBLANES, D_LRU)
        b = b_s[rows_of(i), :].reshape(groups, SUBLANES, D_LRU)
        row = lax.broadcasted_iota(jnp.int32, (1, SUBLANES, D_LRU), 1)
        for d in (1, 2, 4):
            keep = row >= d
            b = jnp.where(keep, a * pltpu.roll(b, d, axis=1) + b, b)
            a = jnp.where(keep, a * pltpu.roll(a, d, axis=1), a)
        for g in range(groups):
            hg = a[g] * carry + b[g]
            r0 = i * SLAB + g * SUBLANES
            hbuf[r0:r0 + SUBLANES, :] = hg
            carry = hg[SUBLANES - 1:SUBLANES, :]
        return carry

    def gated_slab(i):
        rec = hbuf[rows_of(i), :] * jax.nn.gelu(gbuf[rows_of(i), :])
        m_s[rows_of(i), D_ATT:D_MODEL] = _rmsnorm(rec, g_lru_ref[...]).astype(BF16)

    n_slabs = TS // SLAB
    state = {"carry": hcar[0:1, :]}

    def lru_slab(i):
        coef_slab(i)
        state["carry"] = scan_slab(i, state["carry"])
        gated_slab(i)

    proj_q()
    for i in range(n_slabs):
        conv_slab(i)
    for c in range(4):
        gate_piece(c)
    proj_kv(kbuf, D_ATT)
    for i in range(n_slabs // 2):
        lru_slab(i)
        if i == 0:
            proj_kv(vbuf, 2 * D_ATT)

    def between(j):
        if j < n_slabs - n_slabs // 2:
            lru_slab(n_slabs // 2 + j)

    _attention(q_s, kbuf, vbuf, bias_ref, abuf, between)
    hcar[0:1, :] = state["carry"]
    ubuf[0:SUBLANES, :] = ubuf[TS:TS + SUBLANES, :]
    kbuf[0:LEFT, :] = kbuf[TS:TS + LEFT, :]
    vbuf[0:LEFT, :] = vbuf[TS:TS + LEFT, :]

    m_s[:, 0:D_ATT] = _rmsnorm(abuf[...], g_att_ref[...]).astype(BF16)
    o_ref[...] = x + _dot(m_s[...], w_out_ref[...])


def _const_spec(shape):
    nd = len(shape)
    return pl.BlockSpec(shape, lambda *_: (0,) * nd, pipeline_mode=pl.Buffered(1))


def _cast_spec(shape, nt, steps):
    rows = next(r for r in range(16, shape[0] + 1, 16)
                if shape[0] % r == 0 and shape[0] // r <= steps)
    last = shape[0] // rows - 1
    return pl.BlockSpec((rows, shape[1]), lambda b, t: (jnp.minimum(b * nt + t, last), 0))


def _mixer(x, g_mix, w_in, bias_tbl, conv_w, conv_b, w_rg, b_rg, w_ig, b_ig, lru_l,
           g_att, g_lru, w_out, next_weights):
    B, S, D = x.shape
    nt = S // TS
    tile = pl.BlockSpec((None, TS, D), lambda b, t: (b, t, 0))
    args = (g_mix, w_in, bias_tbl, conv_w, conv_b, w_rg, b_rg, w_ig, b_ig, lru_l,
            g_att, g_lru, w_out)
    cast_specs = [_cast_spec(w.shape, nt, B * nt) for w in next_weights]
    kv = pltpu.VMEM((LEFT + TS, 2 * D_ATT), BF16)
    lru = pltpu.VMEM((TS, D_LRU), F32)
    return pl.pallas_call(
        _mixer_kernel,
        grid=(B, S // TS),
        in_specs=[tile] + [_const_spec(a.shape) for a in args] + cast_specs,
        out_specs=[tile] + cast_specs,
        out_shape=[jax.ShapeDtypeStruct((B, S, D), F32)]
                  + [jax.ShapeDtypeStruct(w.shape, BF16) for w in next_weights],
        scratch_shapes=[
            kv,
            kv,
            pltpu.VMEM((TS, D_ATT), BF16),
            pltpu.VMEM((SUBLANES + TS, D_LRU), F32),
            lru,
            lru,
            pltpu.VMEM((TS, D_LRU), BF16),
            lru, lru,
            lru, lru,
            lru,
            pltpu.VMEM((TS, D_ATT), F32),
            pltpu.VMEM((TS, D_MODEL), BF16),
            pltpu.VMEM((SUBLANES, D_LRU), F32),
        ],
        compiler_params=pltpu.CompilerParams(
            dimension_semantics=("arbitrary", "arbitrary"), vmem_limit_bytes=VMEM_LIMIT),
        name="mixer",
    )(x, *args, *next_weights)


def _cross_ffn_kernel(x_ref, kx_ref, vx_ref, g_cross_ref, wq_ref, wo_ref,
                      g_ffn_ref, w_gate_ref, w_up_ref, w_down_ref, g_final_ref, o_ref):
    def half(rows):
        x = x_ref[rows, :]
        hc = _rmsnorm(x, g_cross_ref[...]).astype(BF16)
        yield
        q = _dot(hc, wq_ref[...]).astype(BF16)
        yield

        def scores(hd):
            cols = slice(hd * X_HEAD_DIM, (hd + 1) * X_HEAD_DIM)
            return _dot_t(q[:, cols], kx_ref[:, cols])

        heads = []
        for hd in range(X_HEADS):
            cols = slice(hd * X_HEAD_DIM, (hd + 1) * X_HEAD_DIM)
            s = scores(hd)
            yield
            p = jnp.exp2(s - jnp.max(s, axis=-1, keepdims=True))
            l = jnp.sum(p, axis=-1, keepdims=True)
            yield
            heads.append((_dot(p.astype(BF16), vx_ref[:, cols]) / l).astype(BF16))
        x = x + _dot(jnp.concatenate(heads, axis=1), wo_ref[...])
        yield
        hf = _rmsnorm(x, g_ffn_ref[...]).astype(BF16)
        yield
        y = x
        for lo, hi in FF_SPLITS:
            gate = _dot(hf, w_gate_ref[:, lo:hi])
            up = _dot(hf, w_up_ref[:, lo:hi])
            yield
            act = (jax.nn.silu(gate) * up).astype(BF16)
            yield
            y = y + _dot(act, w_down_ref[lo:hi, :])
        yield
        o_ref[rows, :] = _rmsnorm(y, g_final_ref[...])

    n_half = 2
    rows_per = TM // n_half
    gens = [half(slice(k * rows_per, (k + 1) * rows_per)) for k in range(n_half)]
    next(gens[0])
    live = list(gens)
    while live:
        for g in list(live):
            try:
                next(g)
            except StopIteration:
                live.remove(g)


def _cross_ffn(x, kx, vx, g_cross, wq, wo, g_ffn, w_gate, w_up, w_down, g_final):
    B, S, D = x.shape
    M = kx.shape[1]
    tile = pl.BlockSpec((None, TM, D), lambda b, t: (b, t, 0))
    memspec = pl.BlockSpec((None, M, D), lambda b, t: (b, 0, 0))
    args = (g_cross, wq, wo, g_ffn, w_gate, w_up, w_down, g_final)
    return pl.pallas_call(
        _cross_ffn_kernel,
        grid=(B, S // TM),
        in_specs=[tile, memspec, memspec] + [_const_spec(a.shape) for a in args],
        out_specs=tile,
        out_shape=jax.ShapeDtypeStruct((B, S, D), F32),
        compiler_params=pltpu.CompilerParams(
            dimension_semantics=("arbitrary", "arbitrary"), vmem_limit_bytes=VMEM_LIMIT),
        name="cross_ffn",
    )(x, kx, vx, *args)


def _bias_table(rel_bias):
    n_heads = rel_bias.shape[0]
    rb = rel_bias.astype(F32) * LOG2E
    slot = np.arange(CWIN) // CHUNK
    tbl = []
    for parity in range(2):
        rel_max = (parity + LEFT_CHUNKS) * CHUNK + (CHUNK - 1)
        rel_min = (parity + LEFT_CHUNKS) * CHUNK - (CWIN - 1)
        hi, lo = min(rel_max, MAX_REL), max(rel_min, -MAX_REL)
        vec = jnp.concatenate([
            jnp.broadcast_to(rb[:, 2 * MAX_REL:], (n_heads, rel_max - hi)),
            jnp.flip(rb[:, lo + MAX_REL:hi + MAX_REL + 1], axis=1),
            jnp.broadcast_to(rb[:, :1], (n_heads, lo - rel_min)),
        ], axis=1)
        rows = jnp.stack([vec[:, CHUNK - 1 - qi:CHUNK - 1 - qi + CWIN] for qi in range(CHUNK)],
                         axis=1)
        band = (slot >= parity) & (slot <= parity + LEFT_CHUNKS)
        far = rb[:, 2 * MAX_REL:, None]
        tbl.append(jnp.where(band[None, None, :], rows - far, NEG))
    return jnp.stack(tbl, axis=1)


def _block_diag(w):
    per = LRU_BLOCKS // 2
    halves = []
    for hlf in range(2):
        rows = []
        for i in range(per):
            blocks = [w[hlf * per + i] if j == i else jnp.zeros((LRU_BLOCK, LRU_BLOCK), w.dtype)
                      for j in range(per)]
            rows.append(jnp.concatenate(blocks, axis=1))
        halves.append(jnp.concatenate(rows, axis=0))
    return jnp.stack(halves).astype(BF16)


def kernel(x, mem, g_mix, w_in, rel_bias, conv_w, conv_b, w_rg, b_rg, w_ig, b_ig, lru_L,
           g_out_attn, g_out_lru, w_out, g_cross, g_mem, wq_c, wk_c, wv_c, wo_c,
           g_ffn, w_gate, w_up, w_down, g_final):
    depth = g_mix.shape[0]
    row = lambda v: v.reshape(1, -1)
    for l in range(depth):
        kx, vx, w_in_bf, w_out_bf = _mem_proj(mem, row(g_mem[l]), wk_c[l], wv_c[l], w_in[l], w_out[l])
        x, wq_bf, wo_bf, w_gate_bf, w_up_bf, w_down_bf = _mixer(
            x, row(g_mix[l]), w_in_bf, _bias_table(rel_bias[l]),
            conv_w[l], row(conv_b[l]), _block_diag(-LOG2E * w_rg[l]), row(-LOG2E * b_rg[l]),
            _block_diag(-LOG2E * w_ig[l]), row(-LOG2E * b_ig[l]), row(lru_L[l]),
            row(g_out_attn[l]), row(g_out_lru[l]), w_out_bf,
            (wq_c[l], wo_c[l], w_gate[l], w_up[l], w_down[l]))
        assert depth == 1
        x = _cross_ffn(x, kx, vx, row(g_cross[l]), wq_bf, wo_bf, row(g_ffn[l]),
                       w_gate_bf, w_up_bf, w_down_bf, row(g_final))
    return x
```

```python
import functools

import numpy as np
import jax
import jax.numpy as jnp
from jax import lax
from jax.experimental import pallas as pl
from jax.experimental.pallas import tpu as pltpu

D_MODEL = 1024
CHUNK = 64
N_MEM = 256
ATT_HEADS = 8
HEAD_DIM = 64
D_ATT = ATT_HEADS * HEAD_DIM
D_LRU = D_MODEL - D_ATT
LRU_BLOCKS = 8
LRU_BLOCK = D_LRU // LRU_BLOCKS
CONV_W = 4
LRU_C = 8.0
LEFT_CHUNKS = 8
MAX_REL = 128
X_HEADS = 4
X_HEAD_DIM = D_MODEL // X_HEADS
D_FF = 2816
D_IN = 3 * D_ATT + 2 * D_LRU
EPS = 1e-6
NEG = -1e30
LOG2E = 1.4426950408889634

LANES = 128
SUBLANES = 8
LEFT = LEFT_CHUNKS * CHUNK
QB = 4 * CHUNK
KWIN = QB + LEFT
CWIN = 10 * CHUNK
HEADS_PER_GROUP = LANES // HEAD_DIM
N_GROUPS = D_ATT // LANES
TS = 512
SLAB = 128
MXU_N = 256
TM = 1024
FF_SPLITS = ((0, 1024), (1024, 2048), (2048, D_FF))
VMEM_CAPACITY = 64 * 1024 * 1024
VMEM_LIMIT = VMEM_CAPACITY - 4 * 1024 * 1024

BF16 = jnp.bfloat16
F32 = jnp.float32


def _rmsnorm(x, g):
    return x * lax.rsqrt(jnp.mean(x * x, axis=-1, keepdims=True) + EPS) * g


def _dot(a, b):
    return jnp.dot(a, b, preferred_element_type=F32)


def _dot_t(a, b):
    return lax.dot_general(a, b, (((1,), (1,)), ((), ())), preferred_element_type=F32)


def _mem_kernel(mem_ref, g_ref, wk_ref, wv_ref, w_in_ref, w_out_ref,
                kx_ref, vx_ref, w_in_bf_ref, w_out_bf_ref):
    mn = _rmsnorm(mem_ref[...], g_ref[...]).astype(BF16)
    kx_ref[...] = (_dot(mn, wk_ref[...].astype(BF16)) * (X_HEAD_DIM ** -0.5 * LOG2E)).astype(BF16)
    vx_ref[...] = _dot(mn, wv_ref[...].astype(BF16)).astype(BF16)
    w_in_bf_ref[...] = w_in_ref[...].astype(BF16)
    w_out_bf_ref[...] = w_out_ref[...].astype(BF16)


def _mem_proj(mem, g_mem, wk, wv, w_in, w_out):
    B, M, D = mem.shape
    const = lambda b: (0, 0)
    rows = lambda b: (b, 0)
    per_mem = pl.BlockSpec((None, M, D), lambda b: (b, 0, 0))
    w_in_rows = pl.BlockSpec((w_in.shape[0] // B, w_in.shape[1]), rows)
    w_out_rows = pl.BlockSpec((w_out.shape[0] // B, w_out.shape[1]), rows)
    return pl.pallas_call(
        _mem_kernel,
        grid=(B,),
        in_specs=[
            per_mem,
            pl.BlockSpec((1, D), const),
            pl.BlockSpec((D, D), const, pipeline_mode=pl.Buffered(1)),
            pl.BlockSpec((D, D), const, pipeline_mode=pl.Buffered(1)),
            w_in_rows,
            w_out_rows,
        ],
        out_specs=[per_mem, per_mem, w_in_rows, w_out_rows],
        out_shape=[jax.ShapeDtypeStruct((B, M, D), BF16)] * 2
                  + [jax.ShapeDtypeStruct(w_in.shape, BF16), jax.ShapeDtypeStruct(w_out.shape, BF16)],
        compiler_params=pltpu.CompilerParams(
            dimension_semantics=("arbitrary",), vmem_limit_bytes=VMEM_LIMIT),
        name="mem_proj",
    )(mem, g_mem, wk, wv, w_in, w_out)


def _head_mask(e):
    lane = lax.broadcasted_iota(jnp.int32, (1, LANES), 1)
    return (lane // HEAD_DIM) == e


_BIAS_BLOCKS = ((3, 4), (0, 3, 4))


def _scores(q_s, kbuf, qb, grp):
    lane = lax.broadcasted_iota(jnp.int32, (1, LANES), 1)
    flag = jnp.broadcast_to(jnp.where(lane == 0, 1.0, 0.0).astype(BF16), (QB, LANES))
    qg = q_s[qb * QB:(qb + 1) * QB, grp * LANES:(grp + 1) * LANES]
    lhs = jnp.concatenate(
        [jnp.concatenate([jnp.where(_head_mask(e), qg, jnp.zeros_like(qg)), flag], axis=1)
         for e in range(HEADS_PER_GROUP)], axis=0)
    return _dot_t(lhs, kbuf[qb * QB:qb * QB + KWIN, grp * 2 * LANES:(grp + 1) * 2 * LANES])


def _probs(s, bias_ref, grp):
    p_rows = []
    for e in range(HEADS_PER_GROUP):
        head = grp * HEADS_PER_GROUP + e
        for i in range(QB // CHUNK):
            w0 = (i // 2) * LANES
            rows = slice(e * QB + i * CHUNK, e * QB + (i + 1) * CHUNK)
            blocks = []
            for blk in range(CWIN // LANES):
                sb = s[rows, w0 + blk * LANES:w0 + (blk + 1) * LANES]
                if blk in _BIAS_BLOCKS[i % 2]:
                    sb = sb + bias_ref[head, i % 2, :, blk * LANES:(blk + 1) * LANES]
                blocks.append(sb)
            si = jnp.concatenate(blocks, axis=1)
            p = jnp.exp2(si - jnp.max(si, axis=-1, keepdims=True)).astype(BF16)
            pad = jnp.zeros((CHUNK, KWIN - CWIN), BF16)
            p_rows.append(jnp.concatenate([p, pad] if w0 == 0 else [pad, p], axis=1))
    return jnp.concatenate(p_rows, axis=0)


def _weighted_values(p, vbuf, qb, grp):
    o = _dot(p, vbuf[qb * QB:qb * QB + KWIN, grp * 2 * LANES:(grp + 1) * 2 * LANES])
    o = o[:, 0:LANES] / o[:, LANES:2 * LANES]
    return jnp.where(_head_mask(0), o[0:QB], o[QB:2 * QB])


def _attention(q_s, kbuf, vbuf, bias_ref, abuf, between):
    order = [(qb, grp) for qb in range(TS // QB) for grp in range(N_GROUPS)]
    s_next = _scores(q_s, kbuf, *order[0])
    for j, (qb, grp) in enumerate(order):
        s_cur = s_next
        if j + 1 < len(order):
            s_next = _scores(q_s, kbuf, *order[j + 1])
        p = _probs(s_cur, bias_ref, grp)
        abuf[qb * QB:(qb + 1) * QB, grp * LANES:(grp + 1) * LANES] = (
            _weighted_values(p, vbuf, qb, grp))
        between(j)


def _mixer_kernel(x_ref, g_mix_ref, w_in_ref, bias_ref, conv_w_ref, conv_b_ref,
                  w_rg_ref, b_rg_ref, w_ig_ref, b_ig_ref, lru_l_ref,
                  g_att_ref, g_lru_ref, w_out_ref,
                  wq_ref, wo_ref, w_gate_ref, w_up_ref, w_down_ref,
                  o_ref,
                  wq_bf_ref, wo_bf_ref, w_gate_bf_ref, w_up_bf_ref, w_down_bf_ref,
                  kbuf, vbuf, q_s, ubuf, gbuf, xc_s, xcb_s, pr_s, pi_s,
                  a_s, b_s, hbuf, abuf, m_s, hcar):
    t = pl.program_id(1)

    @pl.when(t == 0)
    def _():
        lane = lax.broadcasted_iota(jnp.int32, (LEFT + TS, 2 * D_ATT), 1)
        row = lax.broadcasted_iota(jnp.int32, (LEFT + TS, 2 * D_ATT), 0)
        const_lanes = (lane // LANES) % 2 == 1
        kbuf[...] = jnp.where(const_lanes & (row < LEFT), NEG, 0.0).astype(BF16)
        vbuf[...] = jnp.where(const_lanes, 1.0, 0.0).astype(BF16)
        ubuf[0:SUBLANES, :] = jnp.zeros((SUBLANES, D_LRU), F32)
        hcar[...] = jnp.zeros_like(hcar)

    x = x_ref[...]
    h = _rmsnorm(x, g_mix_ref[...]).astype(BF16)
    ubuf[SUBLANES:SUBLANES + TS, :] = _dot(h, w_in_ref[:, 3 * D_ATT:3 * D_ATT + D_LRU])
    gbuf[...] = _dot(h, w_in_ref[:, 3 * D_ATT + D_LRU:D_IN])

    def proj_q():
        q_s[...] = (_dot(h, w_in_ref[:, 0:D_ATT]) * (HEAD_DIM ** -0.5 * LOG2E)).astype(BF16)

    def proj_kv(buf, lo):
        f = _dot(h, w_in_ref[:, lo:lo + D_ATT]).astype(BF16)
        for grp in range(N_GROUPS):
            src = slice(grp * LANES, (grp + 1) * LANES)
            dst = slice(grp * 2 * LANES, grp * 2 * LANES + LANES)
            buf[LEFT:LEFT + TS, dst] = f[:, src]

    def rows_of(i):
        return slice(i * SLAB, (i + 1) * SLAB)

    def conv_slab(i):
        xc = conv_b_ref[...]
        for j in range(CONV_W):
            off = i * SLAB + SUBLANES - (CONV_W - 1) + j
            xc = xc + ubuf[off:off + SLAB, :] * conv_w_ref[j:j + 1, :]
        xc_s[rows_of(i), :] = xc
        xcb_s[rows_of(i), :] = xc.astype(BF16)

    def gate_piece(c):
        gate, half = divmod(c, 2)
        cols = slice(half * MXU_N, (half + 1) * MXU_N)
        w_ref, dst = ((w_rg_ref, pr_s), (w_ig_ref, pi_s))[gate]
        dst[:, cols] = _dot(xcb_s[:, cols], w_ref[half])

    neg_l = -lru_l_ref[...]
    decay_rate = (-LRU_C * LOG2E) * (jnp.maximum(neg_l, 0.0) + jnp.log1p(jnp.exp(-jnp.abs(neg_l))))

    def coef_slab(i):
        r = pl.reciprocal(1.0 + jnp.exp2(pr_s[rows_of(i), :] + b_rg_ref[...]))
        ig = pl.reciprocal(1.0 + jnp.exp2(pi_s[rows_of(i), :] + b_ig_ref[...]))
        a = jnp.exp2(decay_rate * r)
        z = jnp.maximum(1.0 - a * a, 0.0)
        mult = jnp.where(z > 0.0, z * lax.rsqrt(z), 0.0)
        a_s[rows_of(i), :] = a
        b_s[rows_of(i), :] = mult * (ig * xc_s[rows_of(i), :])

    def scan_slab(i, carry):
        groups = SLAB // SUBLANES
        a = a_s[rows_of(i), :].reshape(groups, SUBLANES, D_LRU)
        b = b_s[rows_of(i), :].reshape(groups, SUBLANES, D_LRU)
        row = lax.broadcasted_iota(jnp.int32, (1, SUBLANES, D_LRU), 1)
        for d in (1, 2, 4):
            keep = row >= d
            b = jnp.where(keep, a * pltpu.roll(b, d, axis=1) + b, b)
            a = jnp.where(keep, a * pltpu.roll(a, d, axis=1), a)
        for g in range(groups):
            hg = a[g] * carry + b[g]
            r0 = i * SLAB + g * SUBLANES
            hbuf[r0:r0 + SUBLANES, :] = hg
            carry = hg[SUBLANES - 1:SUBLANES, :]
        return carry

    def gated_slab(i):
        rec = hbuf[rows_of(i), :] * jax.nn.gelu(gbuf[rows_of(i), :])
        m_s[rows_of(i), D_ATT:D_MODEL] = _rmsnorm(rec, g_lru_ref[...]).astype(BF16)

    n_slabs = TS // SLAB
    state = {"carry": hcar[0:1, :]}

    def lru_slab(i):
        coef_slab(i)
        state["carry"] = scan_slab(i, state["carry"])
        gated_slab(i)

    proj_q()
    for i in range(n_slabs):
        conv_slab(i)
    for c in range(4):
        gate_piece(c)
    proj_kv(kbuf, D_ATT)
    for i in range(n_slabs // 2):
        lru_slab(i)
        if i == 0:
            proj_kv(vbuf, 2 * D_ATT)

    def between(j):
        if j < n_slabs - n_slabs // 2:
            lru_slab(n_slabs // 2 + j)

    _attention(q_s, kbuf, vbuf, bias_ref, abuf, between)
    hcar[0:1, :] = state["carry"]
    ubuf[0:SUBLANES, :] = ubuf[TS:TS + SUBLANES, :]
    kbuf[0:LEFT, :] = kbuf[TS:TS + LEFT, :]
    vbuf[0:LEFT, :] = vbuf[TS:TS + LEFT, :]

    m_s[:, 0:D_ATT] = _rmsnorm(abuf[...], g_att_ref[...]).astype(BF16)
    o_ref[...] = x + _dot(m_s[...], w_out_ref[...])

    for src, dst in ((wq_ref, wq_bf_ref), (wo_ref, wo_bf_ref), (w_gate_ref, w_gate_bf_ref),
                     (w_up_ref, w_up_bf_ref), (w_down_ref, w_down_bf_ref)):
        dst[...] = src[...].astype(BF16)


def _const_spec(shape):
    nd = len(shape)
    return pl.BlockSpec(shape, lambda *_: (0,) * nd, pipeline_mode=pl.Buffered(1))


def _cast_spec(shape, nt, steps):
    rows = next(r for r in range(16, shape[0] + 1, 16)
                if shape[0] % r == 0 and shape[0] // r <= steps)
    last = shape[0] // rows - 1
    return pl.BlockSpec((rows, shape[1]), lambda b, t: (jnp.minimum(b * nt + t, last), 0))


def _mixer(x, g_mix, w_in, bias_tbl, conv_w, conv_b, w_rg, b_rg, w_ig, b_ig, lru_l,
           g_att, g_lru, w_out, next_weights):
    B, S, D = x.shape
    nt = S // TS
    tile = pl.BlockSpec((None, TS, D), lambda b, t: (b, t, 0))
    args = (g_mix, w_in, bias_tbl, conv_w, conv_b, w_rg, b_rg, w_ig, b_ig, lru_l,
            g_att, g_lru, w_out)
    cast_specs = [_cast_spec(w.shape, nt, B * nt) for w in next_weights]
    kv = pltpu.VMEM((LEFT + TS, 2 * D_ATT), BF16)
    lru = pltpu.VMEM((TS, D_LRU), F32)
    return pl.pallas_call(
        _mixer_kernel,
        grid=(B, S // TS),
        in_specs=[tile] + [_const_spec(a.shape) for a in args] + cast_specs,
        out_specs=[tile] + cast_specs,
        out_shape=[jax.ShapeDtypeStruct((B, S, D), F32)]
                  + [jax.ShapeDtypeStruct(w.shape, BF16) for w in next_weights],
        scratch_shapes=[
            kv,
            kv,
            pltpu.VMEM((TS, D_ATT), BF16),
            pltpu.VMEM((SUBLANES + TS, D_LRU), F32),
            lru,
            lru,
            pltpu.VMEM((TS, D_LRU), BF16),
            lru, lru,
            lru, lru,
            lru,
            pltpu.VMEM((TS, D_ATT), F32),
            pltpu.VMEM((TS, D_MODEL), BF16),
            pltpu.VMEM((SUBLANES, D_LRU), F32),
        ],
        compiler_params=pltpu.CompilerParams(
            dimension_semantics=("arbitrary", "arbitrary"), vmem_limit_bytes=VMEM_LIMIT),
        name="mixer",
    )(x, *args, *next_weights)


def _cross_ffn_kernel(x_ref, kx_ref, vx_ref, g_cross_ref, wq_ref, wo_ref,
                      g_ffn_ref, w_gate_ref, w_up_ref, w_down_ref, g_final_ref, o_ref):
    def half(rows):
        x = x_ref[rows, :]
        hc = _rmsnorm(x, g_cross_ref[...]).astype(BF16)
        yield
        q = _dot(hc, wq_ref[...]).astype(BF16)
        yield

        def scores(hd):
            cols = slice(hd * X_HEAD_DIM, (hd + 1) * X_HEAD_DIM)
            return _dot_t(q[:, cols], kx_ref[:, cols])

        heads = []
        for hd in range(X_HEADS):
            cols = slice(hd * X_HEAD_DIM, (hd + 1) * X_HEAD_DIM)
            s = scores(hd)
            yield
            p = jnp.exp2(s - jnp.max(s, axis=-1, keepdims=True))
            l = jnp.sum(p, axis=-1, keepdims=True)
            yield
            heads.append((_dot(p.astype(BF16), vx_ref[:, cols]) / l).astype(BF16))
        x = x + _dot(jnp.concatenate(heads, axis=1), wo_ref[...])
        yield
        hf = _rmsnorm(x, g_ffn_ref[...]).astype(BF16)
        yield
        y = x
        for lo, hi in FF_SPLITS:
            gate = _dot(hf, w_gate_ref[:, lo:hi])
            up = _dot(hf, w_up_ref[:, lo:hi])
            yield
            act = (jax.nn.silu(gate) * up).astype(BF16)
            yield
            y = y + _dot(act, w_down_ref[lo:hi, :])
        yield
        o_ref[rows, :] = _rmsnorm(y, g_final_ref[...])

    n_half = 2
    rows_per = TM // n_half
    gens = [half(slice(k * rows_per, (k + 1) * rows_per)) for k in range(n_half)]
    next(gens[0])
    live = list(gens)
    while live:
        for g in list(live):
            try:
                next(g)
            except StopIteration:
                live.remove(g)


def _cross_ffn(x, kx, vx, g_cross, wq, wo, g_ffn, w_gate, w_up, w_down, g_final):
    B, S, D = x.shape
    M = kx.shape[1]
    tile = pl.BlockSpec((None, TM, D), lambda b, t: (b, t, 0))
    memspec = pl.BlockSpec((None, M, D), lambda b, t: (b, 0, 0))
    args = (g_cross, wq, wo, g_ffn, w_gate, w_up, w_down, g_final)
    return pl.pallas_call(
        _cross_ffn_kernel,
        grid=(B, S // TM),
        in_specs=[tile, memspec, memspec] + [_const_spec(a.shape) for a in args],
        out_specs=tile,
        out_shape=jax.ShapeDtypeStruct((B, S, D), F32),
        compiler_params=pltpu.CompilerParams(
            dimension_semantics=("arbitrary", "arbitrary"), vmem_limit_bytes=VMEM_LIMIT),
        name="cross_ffn",
    )(x, kx, vx, *args)


def _bias_table(rel_bias):
    n_heads = rel_bias.shape[0]
    rb = rel_bias.astype(F32) * LOG2E
    slot = np.arange(CWIN) // CHUNK
    tbl = []
    for parity in range(2):
        rel_max = (parity + LEFT_CHUNKS) * CHUNK + (CHUNK - 1)
        rel_min = (parity + LEFT_CHUNKS) * CHUNK - (CWIN - 1)
        hi, lo = min(rel_max, MAX_REL), max(rel_min, -MAX_REL)
        vec = jnp.concatenate([
            jnp.broadcast_to(rb[:, 2 * MAX_REL:], (n_heads, rel_max - hi)),
            jnp.flip(rb[:, lo + MAX_REL:hi + MAX_REL + 1], axis=1),
            jnp.broadcast_to(rb[:, :1], (n_heads, lo - rel_min)),
        ], axis=1)
        period = vec.shape[1]
        flat = jnp.tile(vec, (1, CHUNK + 1))[:, :CHUNK * (period + 1)]
        skew = flat.reshape(n_heads, CHUNK, period + 1)[:, :, :CWIN]
        rows = jnp.flip(skew, axis=1)
        band = (slot >= parity) & (slot <= parity + LEFT_CHUNKS)
        far = rb[:, 2 * MAX_REL:, None]
        tbl.append(jnp.where(band[None, None, :], rows - far, NEG))
    return jnp.stack(tbl, axis=1)


def _block_diag(w):
    per = LRU_BLOCKS // 2
    halves = []
    for hlf in range(2):
        rows = []
        for i in range(per):
            blocks = [w[hlf * per + i] if j == i else jnp.zeros((LRU_BLOCK, LRU_BLOCK), w.dtype)
                      for j in range(per)]
            rows.append(jnp.concatenate(blocks, axis=1))
        halves.append(jnp.concatenate(rows, axis=0))
    return jnp.stack(halves).astype(BF16)


def kernel(x, mem, g_mix, w_in, rel_bias, conv_w, conv_b, w_rg, b_rg, w_ig, b_ig, lru_L,
           g_out_attn, g_out_lru, w_out, g_cross, g_mem, wq_c, wk_c, wv_c, wo_c,
           g_ffn, w_gate, w_up, w_down, g_final):
    depth = g_mix.shape[0]
    row = lambda v: v.reshape(1, -1)
    for l in range(depth):
        kx, vx, w_in_bf, w_out_bf = _mem_proj(mem, row(g_mem[l]), wk_c[l], wv_c[l], w_in[l], w_out[l])
        x, wq_bf, wo_bf, w_gate_bf, w_up_bf, w_down_bf = _mixer(
            x, row(g_mix[l]), w_in_bf, _bias_table(rel_bias[l]),
            conv_w[l], row(conv_b[l]), _block_diag(-LOG2E * w_rg[l]), row(-LOG2E * b_rg[l]),
            _block_diag(-LOG2E * w_ig[l]), row(-LOG2E * b_ig[l]), row(lru_L[l]),
            row(g_out_attn[l]), row(g_out_lru[l]), w_out_bf,
            (wq_c[l], wo_c[l], w_gate[l], w_up[l], w_down[l]))
        assert depth == 1
        x = _cross_ffn(x, kx, vx, row(g_cross[l]), wq_bf, wo_bf, row(g_ffn[l]),
                       w_gate_bf, w_up_bf, w_down_bf, row(g_final))
    return x
```

```python
import functools

import numpy as np
import jax
import jax.numpy as jnp
from jax import lax
from jax.experimental import pallas as pl
from jax.experimental.pallas import tpu as pltpu

D_MODEL = 1024
CHUNK = 64
N_MEM = 256
ATT_HEADS = 8
HEAD_DIM = 64
D_ATT = ATT_HEADS * HEAD_DIM
D_LRU = D_MODEL - D_ATT
LRU_BLOCKS = 8
LRU_BLOCK = D_LRU // LRU_BLOCKS
CONV_W = 4
LRU_C = 8.0
LEFT_CHUNKS = 8
MAX_REL = 128
X_HEADS = 4
X_HEAD_DIM = D_MODEL // X_HEADS
D_FF = 2816
D_IN = 3 * D_ATT + 2 * D_LRU
EPS = 1e-6
NEG = -1e30
LOG2E = 1.4426950408889634

LANES = 128
SUBLANES = 8
LEFT = LEFT_CHUNKS * CHUNK
QB = 4 * CHUNK
KWIN = QB + LEFT
CWIN = 10 * CHUNK
HEADS_PER_GROUP = LANES // HEAD_DIM
N_GROUPS = D_ATT // LANES
TS = 1024
SLAB = 128
MXU_N = 256
TM = 1024
FF_SPLITS = ((0, 1024), (1024, 2048), (2048, D_FF))
VMEM_CAPACITY = 64 * 1024 * 1024
VMEM_LIMIT = VMEM_CAPACITY - 4 * 1024 * 1024

BF16 = jnp.bfloat16
F32 = jnp.float32


def _rmsnorm(x, g):
    return x * lax.rsqrt(jnp.mean(x * x, axis=-1, keepdims=True) + EPS) * g


def _dot(a, b):
    return jnp.dot(a, b, preferred_element_type=F32)


def _dot_t(a, b):
    return lax.dot_general(a, b, (((1,), (1,)), ((), ())), preferred_element_type=F32)


def _mem_kernel(mem_ref, g_ref, wk_ref, wv_ref, w_in_ref, w_out_ref,
                kx_ref, vx_ref, w_in_bf_ref, w_out_bf_ref):
    mn = _rmsnorm(mem_ref[...], g_ref[...]).astype(BF16)
    kx_ref[...] = (_dot(mn, wk_ref[...].astype(BF16)) * (X_HEAD_DIM ** -0.5 * LOG2E)).astype(BF16)
    vx_ref[...] = _dot(mn, wv_ref[...].astype(BF16)).astype(BF16)
    w_in_bf_ref[...] = w_in_ref[...].astype(BF16)
    w_out_bf_ref[...] = w_out_ref[...].astype(BF16)


def _mem_proj(mem, g_mem, wk, wv, w_in, w_out):
    B, M, D = mem.shape
    const = lambda b: (0, 0)
    rows = lambda b: (b, 0)
    per_mem = pl.BlockSpec((None, M, D), lambda b: (b, 0, 0))
    w_in_rows = pl.BlockSpec((w_in.shape[0] // B, w_in.shape[1]), rows)
    w_out_rows = pl.BlockSpec((w_out.shape[0] // B, w_out.shape[1]), rows)
    return pl.pallas_call(
        _mem_kernel,
        grid=(B,),
        in_specs=[
            per_mem,
            pl.BlockSpec((1, D), const),
            pl.BlockSpec((D, D), const, pipeline_mode=pl.Buffered(1)),
            pl.BlockSpec((D, D), const, pipeline_mode=pl.Buffered(1)),
            w_in_rows,
            w_out_rows,
        ],
        out_specs=[per_mem, per_mem, w_in_rows, w_out_rows],
        out_shape=[jax.ShapeDtypeStruct((B, M, D), BF16)] * 2
                  + [jax.ShapeDtypeStruct(w_in.shape, BF16), jax.ShapeDtypeStruct(w_out.shape, BF16)],
        compiler_params=pltpu.CompilerParams(
            dimension_semantics=("arbitrary",), vmem_limit_bytes=VMEM_LIMIT),
        name="mem_proj",
    )(mem, g_mem, wk, wv, w_in, w_out)


def _head_mask(e):
    lane = lax.broadcasted_iota(jnp.int32, (1, LANES), 1)
    return (lane // HEAD_DIM) == e


_BIAS_BLOCKS = ((3, 4), (0, 3, 4))


def _scores(q_s, kbuf, qb, grp):
    lane = lax.broadcasted_iota(jnp.int32, (1, LANES), 1)
    flag = jnp.broadcast_to(jnp.where(lane == 0, 1.0, 0.0).astype(BF16), (QB, LANES))
    qg = q_s[qb * QB:(qb + 1) * QB, grp * LANES:(grp + 1) * LANES]
    lhs = jnp.concatenate(
        [jnp.concatenate([jnp.where(_head_mask(e), qg, jnp.zeros_like(qg)), flag], axis=1)
         for e in range(HEADS_PER_GROUP)], axis=0)
    return _dot_t(lhs, kbuf[qb * QB:qb * QB + KWIN, grp * 2 * LANES:(grp + 1) * 2 * LANES])


def _probs(s, bias_ref, grp):
    p_rows = []
    for e in range(HEADS_PER_GROUP):
        head = grp * HEADS_PER_GROUP + e
        for i in range(QB // CHUNK):
            w0 = (i // 2) * LANES
            rows = slice(e * QB + i * CHUNK, e * QB + (i + 1) * CHUNK)
            blocks = []
            for blk in range(CWIN // LANES):
                sb = s[rows, w0 + blk * LANES:w0 + (blk + 1) * LANES]
                if blk in _BIAS_BLOCKS[i % 2]:
                    sb = sb + bias_ref[head, i % 2, :, blk * LANES:(blk + 1) * LANES]
                blocks.append(sb)
            si = jnp.concatenate(blocks, axis=1)
            p = jnp.exp2(si - jnp.max(si, axis=-1, keepdims=True)).astype(BF16)
            pad = jnp.zeros((CHUNK, KWIN - CWIN), BF16)
            p_rows.append(jnp.concatenate([p, pad] if w0 == 0 else [pad, p], axis=1))
    return jnp.concatenate(p_rows, axis=0)


def _weighted_values(p, vbuf, qb, grp):
    o = _dot(p, vbuf[qb * QB:qb * QB + KWIN, grp * 2 * LANES:(grp + 1) * 2 * LANES])
    o = o[:, 0:LANES] / o[:, LANES:2 * LANES]
    return jnp.where(_head_mask(0), o[0:QB], o[QB:2 * QB])


def _attention(q_s, kbuf, vbuf, bias_ref, abuf, between):
    order = [(qb, grp) for qb in range(TS // QB) for grp in range(N_GROUPS)]
    s_next = _scores(q_s, kbuf, *order[0])
    for j, (qb, grp) in enumerate(order):
        s_cur = s_next
        if j + 1 < len(order):
            s_next = _scores(q_s, kbuf, *order[j + 1])
        p = _probs(s_cur, bias_ref, grp)
        abuf[qb * QB:(qb + 1) * QB, grp * LANES:(grp + 1) * LANES] = (
            _weighted_values(p, vbuf, qb, grp))
        between(j)


def _mixer_kernel(x_ref, g_mix_ref, w_in_ref, bias_ref, conv_w_ref, conv_b_ref,
                  w_rg_ref, b_rg_ref, w_ig_ref, b_ig_ref, lru_l_ref,
                  g_att_ref, g_lru_ref, w_out_ref,
                  wq_ref, wo_ref, w_gate_ref, w_up_ref, w_down_ref,
                  o_ref,
                  wq_bf_ref, wo_bf_ref, w_gate_bf_ref, w_up_bf_ref, w_down_bf_ref,
                  kbuf, vbuf, q_s, ubuf, gbuf, xc_s, xcb_s, pr_s, pi_s,
                  a_s, b_s, hbuf, abuf, m_s, hcar):
    t = pl.program_id(1)

    @pl.when(t == 0)
    def _():
        lane = lax.broadcasted_iota(jnp.int32, (LEFT + TS, 2 * D_ATT), 1)
        row = lax.broadcasted_iota(jnp.int32, (LEFT + TS, 2 * D_ATT), 0)
        const_lanes = (lane // LANES) % 2 == 1
        kbuf[...] = jnp.where(const_lanes & (row < LEFT), NEG, 0.0).astype(BF16)
        vbuf[...] = jnp.where(const_lanes, 1.0, 0.0).astype(BF16)
        ubuf[0:SUBLANES, :] = jnp.zeros((SUBLANES, D_LRU), F32)
        hcar[...] = jnp.zeros_like(hcar)

    x = x_ref[...]
    h = _rmsnorm(x, g_mix_ref[...]).astype(BF16)
    ubuf[SUBLANES:SUBLANES + TS, :] = _dot(h, w_in_ref[:, 3 * D_ATT:3 * D_ATT + D_LRU])
    gbuf[...] = _dot(h, w_in_ref[:, 3 * D_ATT + D_LRU:D_IN])

    def proj_q():
        q_s[...] = (_dot(h, w_in_ref[:, 0:D_ATT]) * (HEAD_DIM ** -0.5 * LOG2E)).astype(BF16)

    def proj_kv(buf, lo):
        f = _dot(h, w_in_ref[:, lo:lo + D_ATT]).astype(BF16)
        for grp in range(N_GROUPS):
            src = slice(grp * LANES, (grp + 1) * LANES)
            dst = slice(grp * 2 * LANES, grp * 2 * LANES + LANES)
            buf[LEFT:LEFT + TS, dst] = f[:, src]

    def rows_of(i):
        return slice(i * SLAB, (i + 1) * SLAB)

    def conv_slab(i):
        xc = conv_b_ref[...]
        for j in range(CONV_W):
            off = i * SLAB + SUBLANES - (CONV_W - 1) + j
            xc = xc + ubuf[off:off + SLAB, :] * conv_w_ref[j:j + 1, :]
        xc_s[rows_of(i), :] = xc
        xcb_s[rows_of(i), :] = xc.astype(BF16)

    def gate_piece(c):
        gate, half = divmod(c, 2)
        cols = slice(half * MXU_N, (half + 1) * MXU_N)
        w_ref, dst = ((w_rg_ref, pr_s), (w_ig_ref, pi_s))[gate]
        dst[:, cols] = _dot(xcb_s[:, cols], w_ref[half])

    neg_l = -lru_l_ref[...]
    decay_rate = (-LRU_C * LOG2E) * (jnp.maximum(neg_l, 0.0) + jnp.log1p(jnp.exp(-jnp.abs(neg_l))))

    def coef_slab(i):
        r = pl.reciprocal(1.0 + jnp.exp2(pr_s[rows_of(i), :] + b_rg_ref[...]))
        ig = pl.reciprocal(1.0 + jnp.exp2(pi_s[rows_of(i), :] + b_ig_ref[...]))
        a = jnp.exp2(decay_rate * r)
        z = jnp.maximum(1.0 - a * a, 0.0)
        mult = jnp.where(z > 0.0, z * lax.rsqrt(z), 0.0)
        a_s[rows_of(i), :] = a
        b_s[rows_of(i), :] = mult * (ig * xc_s[rows_of(i), :])

    def scan_slab(i, carry):
        groups = SLAB // SUBLANES
        a = a_s[rows_of(i), :].reshape(groups, SUBLANES, D_LRU)
        b = b_s[rows_of(i), :].reshape(groups, SUBLANES, D_LRU)
        row = lax.broadcasted_iota(jnp.int32, (1, SUBLANES, D_LRU), 1)
        for d in (1, 2, 4):
            keep = row >= d
            b = jnp.where(keep, a * pltpu.roll(b, d, axis=1) + b, b)
            a = jnp.where(keep, a * pltpu.roll(a, d, axis=1), a)
        for g in range(groups):
            hg = a[g] * carry + b[g]
            r0 = i * SLAB + g * SUBLANES
            hbuf[r0:r0 + SUBLANES, :] = hg
            carry = hg[SUBLANES - 1:SUBLANES, :]
        return carry

    def gated_slab(i):
        rec = hbuf[rows_of(i), :] * jax.nn.gelu(gbuf[rows_of(i), :])
        m_s[rows_of(i), D_ATT:D_MODEL] = _rmsnorm(rec, g_lru_ref[...]).astype(BF16)

    n_slabs = TS // SLAB
    state = {"carry": hcar[0:1, :]}

    def lru_slab(i):
        coef_slab(i)
        state["carry"] = scan_slab(i, state["carry"])
        gated_slab(i)

    proj_q()
    for i in range(n_slabs):
        conv_slab(i)
    for c in range(4):
        gate_piece(c)
    proj_kv(kbuf, D_ATT)
    for i in range(n_slabs // 2):
        lru_slab(i)
        if i == 0:
            proj_kv(vbuf, 2 * D_ATT)

    def between(j):
        if j < n_slabs - n_slabs // 2:
            lru_slab(n_slabs // 2 + j)

    _attention(q_s, kbuf, vbuf, bias_ref, abuf, between)
    hcar[0:1, :] = state["carry"]
    ubuf[0:SUBLANES, :] = ubuf[TS:TS + SUBLANES, :]
    kbuf[0:LEFT, :] = kbuf[TS:TS + LEFT, :]
    vbuf[0:LEFT, :] = vbuf[TS:TS + LEFT, :]

    m_s[:, 0:D_ATT] = _rmsnorm(abuf[...], g_att_ref[...]).astype(BF16)
    o_ref[...] = x + _dot(m_s[...], w_out_ref[...])

    for src, dst in ((wq_ref, wq_bf_ref), (wo_ref, wo_bf_ref), (w_gate_ref, w_gate_bf_ref),
                     (w_up_ref, w_up_bf_ref), (w_down_ref, w_down_bf_ref)):
        dst[...] = src[...].astype(BF16)


def _const_spec(shape):
    nd = len(shape)
    return pl.BlockSpec(shape, lambda *_: (0,) * nd, pipeline_mode=pl.Buffered(1))


def _cast_spec(shape, nt, steps):
    rows = next(r for r in range(16, shape[0] + 1, 16)
                if shape[0] % r == 0 and shape[0] // r <= steps)
    last = shape[0] // rows - 1
    return pl.BlockSpec((rows, shape[1]), lambda b, t: (jnp.minimum(b * nt + t, last), 0))


def _mixer(x, g_mix, w_in, bias_tbl, conv_w, conv_b, w_rg, b_rg, w_ig, b_ig, lru_l,
           g_att, g_lru, w_out, next_weights):
    B, S, D = x.shape
    nt = S // TS
    tile = pl.BlockSpec((None, TS, D), lambda b, t: (b, t, 0))
    args = (g_mix, w_in, bias_tbl, conv_w, conv_b, w_rg, b_rg, w_ig, b_ig, lru_l,
            g_att, g_lru, w_out)
    cast_specs = [_cast_spec(w.shape, nt, B * nt) for w in next_weights]
    kv = pltpu.VMEM((LEFT + TS, 2 * D_ATT), BF16)
    lru = pltpu.VMEM((TS, D_LRU), F32)
    return pl.pallas_call(
        _mixer_kernel,
        grid=(B, S // TS),
        in_specs=[tile] + [_const_spec(a.shape) for a in args] + cast_specs,
        out_specs=[tile] + cast_specs,
        out_shape=[jax.ShapeDtypeStruct((B, S, D), F32)]
                  + [jax.ShapeDtypeStruct(w.shape, BF16) for w in next_weights],
        scratch_shapes=[
            kv,
            kv,
            pltpu.VMEM((TS, D_ATT), BF16),
            pltpu.VMEM((SUBLANES + TS, D_LRU), F32),
            lru,
            lru,
            pltpu.VMEM((TS, D_LRU), BF16),
            lru, lru,
            lru, lru,
            lru,
            pltpu.VMEM((TS, D_ATT), F32),
            pltpu.VMEM((TS, D_MODEL), BF16),
            pltpu.VMEM((SUBLANES, D_LRU), F32),
        ],
        compiler_params=pltpu.CompilerParams(
            dimension_semantics=("arbitrary", "arbitrary"), vmem_limit_bytes=VMEM_LIMIT),
        name="mixer",
    )(x, *args, *next_weights)


def _cross_ffn_kernel(x_ref, kx_ref, vx_ref, g_cross_ref, wq_ref, wo_ref,
                      g_ffn_ref, w_gate_ref, w_up_ref, w_down_ref, g_final_ref, o_ref):
    def half(rows):
        x = x_ref[rows, :]
        hc = _rmsnorm(x, g_cross_ref[...]).astype(BF16)
        yield
        q = _dot(hc, wq_ref[...]).astype(BF16)
        yield

        def scores(hd):
            cols = slice(hd * X_HEAD_DIM, (hd + 1) * X_HEAD_DIM)
            return _dot_t(q[:, cols], kx_ref[:, cols])

        heads = []
        for hd in range(X_HEADS):
            cols = slice(hd * X_HEAD_DIM, (hd + 1) * X_HEAD_DIM)
            s = scores(hd)
            yield
            p = jnp.exp2(s - jnp.max(s, axis=-1, keepdims=True))
            l = jnp.sum(p, axis=-1, keepdims=True)
            yield
            heads.append((_dot(p.astype(BF16), vx_ref[:, cols]) / l).astype(BF16))
        x = x + _dot(jnp.concatenate(heads, axis=1), wo_ref[...])
        yield
        hf = _rmsnorm(x, g_ffn_ref[...]).astype(BF16)
        yield
        y = x
        for lo, hi in FF_SPLITS:
            gate = _dot(hf, w_gate_ref[:, lo:hi])
            up = _dot(hf, w_up_ref[:, lo:hi])
            yield
            act = (jax.nn.silu(gate) * up).astype(BF16)
            yield
            y = y + _dot(act, w_down_ref[lo:hi, :])
        yield
        o_ref[rows, :] = _rmsnorm(y, g_final_ref[...])

    n_half = 2
    rows_per = TM // n_half
    gens = [half(slice(k * rows_per, (k + 1) * rows_per)) for k in range(n_half)]
    next(gens[0])
    live = list(gens)
    while live:
        for g in list(live):
            try:
                next(g)
            except StopIteration:
                live.remove(g)


def _cross_ffn(x, kx, vx, g_cross, wq, wo, g_ffn, w_gate, w_up, w_down, g_final):
    B, S, D = x.shape
    M = kx.shape[1]
    tile = pl.BlockSpec((None, TM, D), lambda b, t: (b, t, 0))
    memspec = pl.BlockSpec((None, M, D), lambda b, t: (b, 0, 0))
    args = (g_cross, wq, wo, g_ffn, w_gate, w_up, w_down, g_final)
    return pl.pallas_call(
        _cross_ffn_kernel,
        grid=(B, S // TM),
        in_specs=[tile, memspec, memspec] + [_const_spec(a.shape) for a in args],
        out_specs=tile,
        out_shape=jax.ShapeDtypeStruct((B, S, D), F32),
        compiler_params=pltpu.CompilerParams(
            dimension_semantics=("arbitrary", "arbitrary"), vmem_limit_bytes=VMEM_LIMIT),
        name="cross_ffn",
    )(x, kx, vx, *args)


def _bias_table(rel_bias):
    n_heads = rel_bias.shape[0]
    rb = rel_bias.astype(F32) * LOG2E
    slot = np.arange(CWIN) // CHUNK
    tbl = []
    for parity in range(2):
        rel_max = (parity + LEFT_CHUNKS) * CHUNK + (CHUNK - 1)
        rel_min = (parity + LEFT_CHUNKS) * CHUNK - (CWIN - 1)
        hi, lo = min(rel_max, MAX_REL), max(rel_min, -MAX_REL)
        vec = jnp.concatenate([
            jnp.broadcast_to(rb[:, 2 * MAX_REL:], (n_heads, rel_max - hi)),
            jnp.flip(rb[:, lo + MAX_REL:hi + MAX_REL + 1], axis=1),
            jnp.broadcast_to(rb[:, :1], (n_heads, lo - rel_min)),
        ], axis=1)
        period = vec.shape[1]
        flat = jnp.tile(vec, (1, CHUNK + 1))[:, :CHUNK * (period + 1)]
        skew = flat.reshape(n_heads, CHUNK, period + 1)[:, :, :CWIN]
        rows = jnp.flip(skew, axis=1)
        band = (slot >= parity) & (slot <= parity + LEFT_CHUNKS)
        far = rb[:, 2 * MAX_REL:, None]
        tbl.append(jnp.where(band[None, None, :], rows - far, NEG))
    return jnp.stack(tbl, axis=1)


def _block_diag(w):
    per = LRU_BLOCKS // 2
    halves = []
    for hlf in range(2):
        rows = []
        for i in range(per):
            blocks = [w[hlf * per + i] if j == i else jnp.zeros((LRU_BLOCK, LRU_BLOCK), w.dtype)
                      for j in range(per)]
            rows.append(jnp.concatenate(blocks, axis=1))
        halves.append(jnp.concatenate(rows, axis=0))
    return jnp.stack(halves).astype(BF16)


def kernel(x, mem, g_mix, w_in, rel_bias, conv_w, conv_b, w_rg, b_rg, w_ig, b_ig, lru_L,
           g_out_attn, g_out_lru, w_out, g_cross, g_mem, wq_c, wk_c, wv_c, wo_c,
           g_ffn, w_gate, w_up, w_down, g_final):
    depth = g_mix.shape[0]
    row = lambda v: v.reshape(1, -1)
    for l in range(depth):
        kx, vx, w_in_bf, w_out_bf = _mem_proj(mem, row(g_mem[l]), wk_c[l], wv_c[l], w_in[l], w_out[l])
        x, wq_bf, wo_bf, w_gate_bf, w_up_bf, w_down_bf = _mixer(
            x, row(g_mix[l]), w_in_bf, _bias_table(rel_bias[l]),
            conv_w[l], row(conv_b[l]), _block_diag(-LOG2E * w_rg[l]), row(-LOG2E * b_rg[l]),
            _block_diag(-LOG2E * w_ig[l]), row(-LOG2E * b_ig[l]), row(lru_L[l]),
            row(g_out_attn[l]), row(g_out_lru[l]), w_out_bf,
            (wq_c[l], wo_c[l], w_gate[l], w_up[l], w_down[l]))
        assert depth == 1
        x = _cross_ffn(x, kx, vx, row(g_cross[l]), wq_bf, wo_bf, row(g_ffn[l]),
                       w_gate_bf, w_up_bf, w_down_bf, row(g_final))
    return x
```

```python
import functools

import numpy as np
import jax
import jax.numpy as jnp
from jax import lax
from jax.experimental import pallas as pl
from jax.experimental.pallas import tpu as pltpu

D_MODEL = 1024
CHUNK = 64
N_MEM = 256
ATT_HEADS = 8
HEAD_DIM = 64
D_ATT = ATT_HEADS * HEAD_DIM
D_LRU = D_MODEL - D_ATT
LRU_BLOCKS = 8
LRU_BLOCK = D_LRU // LRU_BLOCKS
CONV_W = 4
LRU_C = 8.0
LEFT_CHUNKS = 8
MAX_REL = 128
X_HEADS = 4
X_HEAD_DIM = D_MODEL // X_HEADS
D_FF = 2816
D_IN = 3 * D_ATT + 2 * D_LRU
EPS = 1e-6
NEG = -1e30
LOG2E = 1.4426950408889634

LANES = 128
SUBLANES = 8
LEFT = LEFT_CHUNKS * CHUNK
QB = 4 * CHUNK
KWIN = QB + LEFT
CWIN = 10 * CHUNK
HEADS_PER_GROUP = LANES // HEAD_DIM
N_GROUPS = D_ATT // LANES
TS = 1024
SLAB = 128
MXU_N = 256
TM = 1024
FF_SPLITS = ((0, 1024), (1024, 2048), (2048, D_FF))
VMEM_CAPACITY = 64 * 1024 * 1024
VMEM_LIMIT = VMEM_CAPACITY - 4 * 1024 * 1024

BF16 = jnp.bfloat16
F32 = jnp.float32


def _rmsnorm(x, g):
    return x * lax.rsqrt(jnp.mean(x * x, axis=-1, keepdims=True) + EPS) * g


def _dot(a, b):
    return jnp.dot(a, b, preferred_element_type=F32)


def _dot_t(a, b):
    return lax.dot_general(a, b, (((1,), (1,)), ((), ())), preferred_element_type=F32)


def _mem_kernel(mem_ref, g_ref, wk_ref, wv_ref, w_in_ref, w_out_ref,
                kx_ref, vx_ref, w_in_bf_ref, w_out_bf_ref):
    mn = _rmsnorm(mem_ref[...], g_ref[...]).astype(BF16)
    kx_ref[...] = (_dot(mn, wk_ref[...].astype(BF16)) * (X_HEAD_DIM ** -0.5 * LOG2E)).astype(BF16)
    vx_ref[...] = _dot(mn, wv_ref[...].astype(BF16)).astype(BF16)
    w_in_bf_ref[...] = w_in_ref[...].astype(BF16)
    w_out_bf_ref[...] = w_out_ref[...].astype(BF16)


def _mem_proj(mem, g_mem, wk, wv, w_in, w_out):
    B, M, D = mem.shape
    const = lambda b: (0, 0)
    rows = lambda b: (b, 0)
    per_mem = pl.BlockSpec((None, M, D), lambda b: (b, 0, 0))
    w_in_rows = pl.BlockSpec((w_in.shape[0] // B, w_in.shape[1]), rows)
    w_out_rows = pl.BlockSpec((w_out.shape[0] // B, w_out.shape[1]), rows)
    return pl.pallas_call(
        _mem_kernel,
        grid=(B,),
        in_specs=[
            per_mem,
            pl.BlockSpec((1, D), const),
            pl.BlockSpec((D, D), const, pipeline_mode=pl.Buffered(1)),
            pl.BlockSpec((D, D), const, pipeline_mode=pl.Buffered(1)),
            w_in_rows,
            w_out_rows,
        ],
        out_specs=[per_mem, per_mem, w_in_rows, w_out_rows],
        out_shape=[jax.ShapeDtypeStruct((B, M, D), BF16)] * 2
                  + [jax.ShapeDtypeStruct(w_in.shape, BF16), jax.ShapeDtypeStruct(w_out.shape, BF16)],
        compiler_params=pltpu.CompilerParams(
            dimension_semantics=("arbitrary",), vmem_limit_bytes=VMEM_LIMIT),
        name="mem_proj",
    )(mem, g_mem, wk, wv, w_in, w_out)


def _head_mask(e):
    lane = lax.broadcasted_iota(jnp.int32, (1, LANES), 1)
    return (lane // HEAD_DIM) == e


_BIAS_BLOCKS = ((3, 4), (0, 3, 4))


def _scores(q_s, kbuf, qb, grp):
    lane = lax.broadcasted_iota(jnp.int32, (1, LANES), 1)
    flag = jnp.broadcast_to(jnp.where(lane == 0, 1.0, 0.0).astype(BF16), (QB, LANES))
    qg = q_s[qb * QB:(qb + 1) * QB, grp * LANES:(grp + 1) * LANES]
    lhs = jnp.concatenate(
        [jnp.concatenate([jnp.where(_head_mask(e), qg, jnp.zeros_like(qg)), flag], axis=1)
         for e in range(HEADS_PER_GROUP)], axis=0)
    return _dot_t(lhs, kbuf[qb * QB:qb * QB + KWIN, grp * 2 * LANES:(grp + 1) * 2 * LANES])


def _probs(s, bias_ref, grp):
    p_rows = []
    for e in range(HEADS_PER_GROUP):
        head = grp * HEADS_PER_GROUP + e
        for i in range(QB // CHUNK):
            w0 = (i // 2) * LANES
            rows = slice(e * QB + i * CHUNK, e * QB + (i + 1) * CHUNK)
            blocks = []
            for blk in range(CWIN // LANES):
                sb = s[rows, w0 + blk * LANES:w0 + (blk + 1) * LANES]
                if blk in _BIAS_BLOCKS[i % 2]:
                    sb = sb + bias_ref[head, i % 2, :, blk * LANES:(blk + 1) * LANES]
                blocks.append(sb)
            si = jnp.concatenate(blocks, axis=1)
            p = jnp.exp2(si - jnp.max(si, axis=-1, keepdims=True)).astype(BF16)
            pad = jnp.zeros((CHUNK, KWIN - CWIN), BF16)
            p_rows.append(jnp.concatenate([p, pad] if w0 == 0 else [pad, p], axis=1))
    return jnp.concatenate(p_rows, axis=0)


def _weighted_values(p, vbuf, qb, grp):
    o = _dot(p, vbuf[qb * QB:qb * QB + KWIN, grp * 2 * LANES:(grp + 1) * 2 * LANES])
    o = o[:, 0:LANES] / o[:, LANES:2 * LANES]
    return jnp.where(_head_mask(0), o[0:QB], o[QB:2 * QB])


def _attention(q_s, kbuf, vbuf, bias_ref, abuf, between):
    order = [(qb, grp) for qb in range(TS // QB) for grp in range(N_GROUPS)]
    s_next = _scores(q_s, kbuf, *order[0])
    for j, (qb, grp) in enumerate(order):
        s_cur = s_next
        if j + 1 < len(order):
            s_next = _scores(q_s, kbuf, *order[j + 1])
        p = _probs(s_cur, bias_ref, grp)
        abuf[qb * QB:(qb + 1) * QB, grp * LANES:(grp + 1) * LANES] = (
            _weighted_values(p, vbuf, qb, grp))
        between(j)


def _mixer_kernel(x_ref, g_mix_ref, w_in_ref, bias_ref, conv_w_ref, conv_b_ref,
                  w_rg_ref, b_rg_ref, w_ig_ref, b_ig_ref, lru_l_ref,
                  g_att_ref, g_lru_ref, w_out_ref,
                  wq_ref, wo_ref, w_gate_ref, w_up_ref, w_down_ref,
                  o_ref,
                  wq_bf_ref, wo_bf_ref, w_gate_bf_ref, w_up_bf_ref, w_down_bf_ref,
                  kbuf, vbuf, q_s, ubuf, gbuf, xc_s, xcb_s, pr_s, pi_s,
                  a_s, b_s, hbuf, abuf, m_s, hcar):
    t = pl.program_id(1)

    @pl.when(t == 0)
    def _():
        lane = lax.broadcasted_iota(jnp.int32, (LEFT + TS, 2 * D_ATT), 1)
        row = lax.broadcasted_iota(jnp.int32, (LEFT + TS, 2 * D_ATT), 0)
        const_lanes = (lane // LANES) % 2 == 1
        kbuf[...] = jnp.where(const_lanes & (row < LEFT), NEG, 0.0).astype(BF16)
        vbuf[...] = jnp.where(const_lanes, 1.0, 0.0).astype(BF16)
        ubuf[0:SUBLANES, :] = jnp.zeros((SUBLANES, D_LRU), F32)
        hcar[...] = jnp.zeros_like(hcar)

    x = x_ref[...]
    h = _rmsnorm(x, g_mix_ref[...]).astype(BF16)
    ubuf[SUBLANES:SUBLANES + TS, :] = _dot(h, w_in_ref[:, 3 * D_ATT:3 * D_ATT + D_LRU])
    gbuf[...] = _dot(h, w_in_ref[:, 3 * D_ATT + D_LRU:D_IN])

    def proj_q():
        q_s[...] = (_dot(h, w_in_ref[:, 0:D_ATT]) * (HEAD_DIM ** -0.5 * LOG2E)).astype(BF16)

    def proj_kv(buf, lo):
        f = _dot(h, w_in_ref[:, lo:lo + D_ATT]).astype(BF16)
        for grp in range(N_GROUPS):
            src = slice(grp * LANES, (grp + 1) * LANES)
            dst = slice(grp * 2 * LANES, grp * 2 * LANES + LANES)
            buf[LEFT:LEFT + TS, dst] = f[:, src]

    def rows_of(i):
        return slice(i * SLAB, (i + 1) * SLAB)

    def conv_slab(i):
        xc = conv_b_ref[...]
        for j in range(CONV_W):
            off = i * SLAB + SUBLANES - (CONV_W - 1) + j
            xc = xc + ubuf[off:off + SLAB, :] * conv_w_ref[j:j + 1, :]
        xc_s[rows_of(i), :] = xc
        xcb_s[rows_of(i), :] = xc.astype(BF16)

    def gate_piece(c):
        gate, half = divmod(c, 2)
        cols = slice(half * MXU_N, (half + 1) * MXU_N)
        w_ref, dst = ((w_rg_ref, pr_s), (w_ig_ref, pi_s))[gate]
        dst[:, cols] = _dot(xcb_s[:, cols], w_ref[half])

    neg_l = -lru_l_ref[...]
    decay_rate = (-LRU_C * LOG2E) * (jnp.maximum(neg_l, 0.0) + jnp.log1p(jnp.exp(-jnp.abs(neg_l))))

    def coef_slab(i):
        r = pl.reciprocal(1.0 + jnp.exp2(pr_s[rows_of(i), :] + b_rg_ref[...]))
        ig = pl.reciprocal(1.0 + jnp.exp2(pi_s[rows_of(i), :] + b_ig_ref[...]))
        a = jnp.exp2(decay_rate * r)
        z = jnp.maximum(1.0 - a * a, 0.0)
        mult = jnp.where(z > 0.0, z * lax.rsqrt(z), 0.0)
        a_s[rows_of(i), :] = a
        b_s[rows_of(i), :] = mult * (ig * xc_s[rows_of(i), :])

    def scan_slab(i, carry):
        groups = SLAB // SUBLANES
        a = a_s[rows_of(i), :].reshape(groups, SUBLANES, D_LRU)
        b = b_s[rows_of(i), :].reshape(groups, SUBLANES, D_LRU)
        row = lax.broadcasted_iota(jnp.int32, (1, SUBLANES, D_LRU), 1)
        for d in (1, 2, 4):
            keep = row >= d
            b = jnp.where(keep, a * pltpu.roll(b, d, axis=1) + b, b)
            a = jnp.where(keep, a * pltpu.roll(a, d, axis=1), a)
        for g in range(groups):
            hg = a[g] * carry + b[g]
            r0 = i * SLAB + g * SUBLANES
            hbuf[r0:r0 + SUBLANES, :] = hg
            carry = hg[SUBLANES - 1:SUBLANES, :]
        return carry

    def gated_slab(i):
        rec = hbuf[rows_of(i), :] * jax.nn.gelu(gbuf[rows_of(i), :])
        m_s[rows_of(i), D_ATT:D_MODEL] = _rmsnorm(rec, g_lru_ref[...]).astype(BF16)

    n_slabs = TS // SLAB
    state = {"carry": hcar[0:1, :]}

    def lru_slab(i):
        coef_slab(i)
        state["carry"] = scan_slab(i, state["carry"])
        gated_slab(i)

    proj_q()
    for i in range(n_slabs):
        conv_slab(i)
    for c in range(4):
        gate_piece(c)
    proj_kv(kbuf, D_ATT)
    for i in range(n_slabs // 2):
        lru_slab(i)
        if i == 0:
            proj_kv(vbuf, 2 * D_ATT)

    def between(j):
        if j < n_slabs - n_slabs // 2:
            lru_slab(n_slabs // 2 + j)

    _attention(q_s, kbuf, vbuf, bias_ref, abuf, between)
    hcar[0:1, :] = state["carry"]
    ubuf[0:SUBLANES, :] = ubuf[TS:TS + SUBLANES, :]
    kbuf[0:LEFT, :] = kbuf[TS:TS + LEFT, :]
    vbuf[0:LEFT, :] = vbuf[TS:TS + LEFT, :]

    m_s[:, 0:D_ATT] = _rmsnorm(abuf[...], g_att_ref[...]).astype(BF16)
    o_ref[...] = x + _dot(m_s[...], w_out_ref[...])

    for src, dst in ((wq_ref, wq_bf_ref), (wo_ref, wo_bf_ref), (w_gate_ref, w_gate_bf_ref),
                     (w_up_ref, w_up_bf_ref), (w_down_ref, w_down_bf_ref)):
        dst[...] = src[...].astype(BF16)


def _const_spec(shape):
    nd = len(shape)
    return pl.BlockSpec(shape, lambda *_: (0,) * nd, pipeline_mode=pl.Buffered(1))


def _cast_spec(shape, nt, steps):
    rows = next(r for r in range(16, shape[0] + 1, 16)
                if shape[0] % r == 0 and shape[0] // r <= steps)
    last = shape[0] // rows - 1
    return pl.BlockSpec((rows, shape[1]), lambda b, t: (jnp.minimum(b * nt + t, last), 0))


def _mixer(x, g_mix, w_in, bias_tbl, conv_w, conv_b, w_rg, b_rg, w_ig, b_ig, lru_l,
           g_att, g_lru, w_out, next_weights):
    B, S, D = x.shape
    nt = S // TS
    tile = pl.BlockSpec((None, TS, D), lambda b, t: (b, t, 0))
    args = (g_mix, w_in, bias_tbl, conv_w, conv_b, w_rg, b_rg, w_ig, b_ig, lru_l,
            g_att, g_lru, w_out)
    cast_specs = [_cast_spec(w.shape, nt, B * nt) for w in next_weights]
    kv = pltpu.VMEM((LEFT + TS, 2 * D_ATT), BF16)
    lru = pltpu.VMEM((TS, D_LRU), F32)
    return pl.pallas_call(
        _mixer_kernel,
        grid=(B, S // TS),
        in_specs=[tile] + [_const_spec(a.shape) for a in args] + cast_specs,
        out_specs=[tile] + cast_specs,
        out_shape=[jax.ShapeDtypeStruct((B, S, D), F32)]
                  + [jax.ShapeDtypeStruct(w.shape, BF16) for w in next_weights],
        scratch_shapes=[
            kv,
            kv,
            pltpu.VMEM((TS, D_ATT), BF16),
            pltpu.VMEM((SUBLANES + TS, D_LRU), F32),
            lru,
            lru,
            pltpu.VMEM((TS, D_LRU), BF16),
            lru, lru,
            lru, lru,
            lru,
            pltpu.VMEM((TS, D_ATT), F32),
            pltpu.VMEM((TS, D_MODEL), BF16),
            pltpu.VMEM((SUBLANES, D_LRU), F32),
        ],
        compiler_params=pltpu.CompilerParams(
            dimension_semantics=("arbitrary", "arbitrary"), vmem_limit_bytes=VMEM_LIMIT),
        name="mixer",
    )(x, *args, *next_weights)


def _cross_ffn_kernel(x_ref, kx_ref, vx_ref, g_cross_ref, wq_ref, wo_ref,
                      g_ffn_ref, w_gate_ref, w_up_ref, w_down_ref, g_final_ref, o_ref):
    def half(rows):
        x = x_ref[rows, :]
        hc = _rmsnorm(x, g_cross_ref[...]).astype(BF16)
        yield
        q = _dot(hc, wq_ref[...]).astype(BF16)
        yield

        def scores(hd):
            cols = slice(hd * X_HEAD_DIM, (hd + 1) * X_HEAD_DIM)
            return _dot_t(q[:, cols], kx_ref[:, cols])

        heads = []
        for hd in range(X_HEADS):
            cols = slice(hd * X_HEAD_DIM, (hd + 1) * X_HEAD_DIM)
            s = scores(hd)
            yield
            p = jnp.exp2(s - jnp.max(s, axis=-1, keepdims=True))
            l = jnp.sum(p, axis=-1, keepdims=True)
            yield
            heads.append((_dot(p.astype(BF16), vx_ref[:, cols]) / l).astype(BF16))
        x = x + _dot(jnp.concatenate(heads, axis=1), wo_ref[...])
        yield
        hf = _rmsnorm(x, g_ffn_ref[...]).astype(BF16)
        yield
        y = x
        for lo, hi in FF_SPLITS:
            gate = _dot(hf, w_gate_ref[:, lo:hi])
            up = _dot(hf, w_up_ref[:, lo:hi])
            yield
            act = (jax.nn.silu(gate) * up).astype(BF16)
            yield
            y = y + _dot(act, w_down_ref[lo:hi, :])
        yield
        o_ref[rows, :] = _rmsnorm(y, g_final_ref[...])

    n_half = 2
    rows_per = TM // n_half
    gens = [half(slice(k * rows_per, (k + 1) * rows_per)) for k in range(n_half)]
    next(gens[0])
    live = list(gens)
    while live:
        for g in list(live):
            try:
                next(g)
            except StopIteration:
                live.remove(g)


def _cross_ffn(x, kx, vx, g_cross, wq, wo, g_ffn, w_gate, w_up, w_down, g_final):
    B, S, D = x.shape
    M = kx.shape[1]
    tile = pl.BlockSpec((None, TM, D), lambda b, t: (b, t, 0))
    memspec = pl.BlockSpec((None, M, D), lambda b, t: (b, 0, 0))
    args = (g_cross, wq, wo, g_ffn, w_gate, w_up, w_down, g_final)
    return pl.pallas_call(
        _cross_ffn_kernel,
        grid=(B, S // TM),
        in_specs=[tile, memspec, memspec] + [_const_spec(a.shape) for a in args],
        out_specs=tile,
        out_shape=jax.ShapeDtypeStruct((B, S, D), F32),
        compiler_params=pltpu.CompilerParams(
            dimension_semantics=("arbitrary", "arbitrary"), vmem_limit_bytes=VMEM_LIMIT),
        name="cross_ffn",
    )(x, kx, vx, *args)


def _bias_table(rel_bias):
    n_heads = rel_bias.shape[0]
    rb = rel_bias.astype(F32) * LOG2E
    slot = np.arange(CWIN) // CHUNK
    tbl = []
    for parity in range(2):
        rel_max = (parity + LEFT_CHUNKS) * CHUNK + (CHUNK - 1)
        rel_min = (parity + LEFT_CHUNKS) * CHUNK - (CWIN - 1)
        hi, lo = min(rel_max, MAX_REL), max(rel_min, -MAX_REL)
        vec = jnp.concatenate([
            jnp.broadcast_to(rb[:, 2 * MAX_REL:], (n_heads, rel_max - hi)),
            jnp.flip(rb[:, lo + MAX_REL:hi + MAX_REL + 1], axis=1),
            jnp.broadcast_to(rb[:, :1], (n_heads, lo - rel_min)),
        ], axis=1)
        period = CWIN + 2 * CHUNK
        vecp = jnp.pad(vec, ((0, 0), (0, period - vec.shape[1])))
        flat = jnp.tile(vecp, (1, CHUNK))[:, :CHUNK * (period - 1)]
        rows = flat.reshape(n_heads, CHUNK, period - 1)[:, :, CHUNK - 1:CHUNK - 1 + CWIN]
        band = (slot >= parity) & (slot <= parity + LEFT_CHUNKS)
        far = rb[:, 2 * MAX_REL:, None]
        tbl.append(jnp.where(band[None, None, :], rows - far, NEG))
    return jnp.stack(tbl, axis=1)


def _block_diag(w):
    per = LRU_BLOCKS // 2
    halves = []
    for hlf in range(2):
        rows = []
        for i in range(per):
            blocks = [w[hlf * per + i] if j == i else jnp.zeros((LRU_BLOCK, LRU_BLOCK), w.dtype)
                      for j in range(per)]
            rows.append(jnp.concatenate(blocks, axis=1))
        halves.append(jnp.concatenate(rows, axis=0))
    return jnp.stack(halves).astype(BF16)


def kernel(x, mem, g_mix, w_in, rel_bias, conv_w, conv_b, w_rg, b_rg, w_ig, b_ig, lru_L,
           g_out_attn, g_out_lru, w_out, g_cross, g_mem, wq_c, wk_c, wv_c, wo_c,
           g_ffn, w_gate, w_up, w_down, g_final):
    depth = g_mix.shape[0]
    row = lambda v: v.reshape(1, -1)
    for l in range(depth):
        kx, vx, w_in_bf, w_out_bf = _mem_proj(mem, row(g_mem[l]), wk_c[l], wv_c[l], w_in[l], w_out[l])
        x, wq_bf, wo_bf, w_gate_bf, w_up_bf, w_down_bf = _mixer(
            x, row(g_mix[l]), w_in_bf, _bias_table(rel_bias[l]),
            conv_w[l], row(conv_b[l]), _block_diag(-LOG2E * w_rg[l]), row(-LOG2E * b_rg[l]),
            _block_diag(-LOG2E * w_ig[l]), row(-LOG2E * b_ig[l]), row(lru_L[l]),
            row(g_out_attn[l]), row(g_out_lru[l]), w_out_bf,
            (wq_c[l], wo_c[l], w_gate[l], w_up[l], w_down[l]))
        assert depth == 1
        x = _cross_ffn(x, kx, vx, row(g_cross[l]), wq_bf, wo_bf, row(g_ffn[l]),
                       w_gate_bf, w_up_bf, w_down_bf, row(g_final))
    return x
```

```python
import functools

import numpy as np
import jax
import jax.numpy as jnp
from jax import lax
from jax.experimental import pallas as pl
from jax.experimental.pallas import tpu as pltpu

D_MODEL = 1024
CHUNK = 64
N_MEM = 256
ATT_HEADS = 8
HEAD_DIM = 64
D_ATT = ATT_HEADS * HEAD_DIM
D_LRU = D_MODEL - D_ATT
LRU_BLOCKS = 8
LRU_BLOCK = D_LRU // LRU_BLOCKS
CONV_W = 4
LRU_C = 8.0
LEFT_CHUNKS = 8
MAX_REL = 128
X_HEADS = 4
X_HEAD_DIM = D_MODEL // X_HEADS
D_FF = 2816
D_IN = 3 * D_ATT + 2 * D_LRU
EPS = 1e-6
NEG = -1e30
LOG2E = 1.4426950408889634

LANES = 128
SUBLANES = 8
LEFT = LEFT_CHUNKS * CHUNK
QB = 4 * CHUNK
KWIN = QB + LEFT
CWIN = 10 * CHUNK
HEADS_PER_GROUP = LANES // HEAD_DIM
N_GROUPS = D_ATT // LANES
TS = 1024
SLAB = 128
MXU_N = 256
TM = 1024
FF_SPLITS = ((0, 1024), (1024, 2048), (2048, D_FF))
VMEM_CAPACITY = 64 * 1024 * 1024
VMEM_LIMIT = VMEM_CAPACITY - 4 * 1024 * 1024

BF16 = jnp.bfloat16
F32 = jnp.float32


def _rmsnorm(x, g):
    return x * lax.rsqrt(jnp.mean(x * x, axis=-1, keepdims=True) + EPS) * g


def _dot(a, b):
    return jnp.dot(a, b, preferred_element_type=F32)


def _dot_t(a, b):
    return lax.dot_general(a, b, (((1,), (1,)), ((), ())), preferred_element_type=F32)


def _mem_kernel(mem_ref, g_ref, wk_ref, wv_ref, w_in_ref, g_in_ref, w_out_ref,
                kx_ref, vx_ref, w_in_bf_ref, w_out_bf_ref):
    mn = _rmsnorm(mem_ref[...], g_ref[...]).astype(BF16)
    kx_ref[...] = (_dot(mn, wk_ref[...].astype(BF16)) * (X_HEAD_DIM ** -0.5 * LOG2E)).astype(BF16)
    vx_ref[...] = _dot(mn, wv_ref[...].astype(BF16)).astype(BF16)
    w_in_bf_ref[...] = (w_in_ref[...] * g_in_ref[...]).astype(BF16)
    w_out_bf_ref[...] = w_out_ref[...].astype(BF16)


def _mem_proj(mem, g_mem, wk, wv, w_in, g_in, w_out):
    B, M, D = mem.shape
    const = lambda b: (0, 0)
    rows = lambda b: (b, 0)
    per_mem = pl.BlockSpec((None, M, D), lambda b: (b, 0, 0))
    w_in_rows = pl.BlockSpec((w_in.shape[0] // B, w_in.shape[1]), rows)
    w_out_rows = pl.BlockSpec((w_out.shape[0] // B, w_out.shape[1]), rows)
    return pl.pallas_call(
        _mem_kernel,
        grid=(B,),
        in_specs=[
            per_mem,
            pl.BlockSpec((1, D), const),
            pl.BlockSpec((D, D), const, pipeline_mode=pl.Buffered(1)),
            pl.BlockSpec((D, D), const, pipeline_mode=pl.Buffered(1)),
            w_in_rows,
            pl.BlockSpec((w_in.shape[0] // B, 1), rows),
            w_out_rows,
        ],
        out_specs=[per_mem, per_mem, w_in_rows, w_out_rows],
        out_shape=[jax.ShapeDtypeStruct((B, M, D), BF16)] * 2
                  + [jax.ShapeDtypeStruct(w_in.shape, BF16), jax.ShapeDtypeStruct(w_out.shape, BF16)],
        compiler_params=pltpu.CompilerParams(
            dimension_semantics=("arbitrary",), vmem_limit_bytes=VMEM_LIMIT),
        name="mem_proj",
    )(mem, g_mem, wk, wv, w_in, g_in, w_out)


def _head_mask(e):
    lane = lax.broadcasted_iota(jnp.int32, (1, LANES), 1)
    return (lane // HEAD_DIM) == e


_BIAS_BLOCKS = ((3, 4), (0, 3, 4))


def _scores(q_s, kbuf, qb, grp):
    lane = lax.broadcasted_iota(jnp.int32, (1, LANES), 1)
    flag = jnp.broadcast_to(jnp.where(lane == 0, 1.0, 0.0).astype(BF16), (QB, LANES))
    qg = q_s[qb * QB:(qb + 1) * QB, grp * LANES:(grp + 1) * LANES]
    lhs = jnp.concatenate(
        [jnp.concatenate([jnp.where(_head_mask(e), qg, jnp.zeros_like(qg)), flag], axis=1)
         for e in range(HEADS_PER_GROUP)], axis=0)
    return _dot_t(lhs, kbuf[qb * QB:qb * QB + KWIN, grp * 2 * LANES:(grp + 1) * 2 * LANES])


def _probs(s, bias_ref, grp):
    p_rows = []
    for e in range(HEADS_PER_GROUP):
        head = grp * HEADS_PER_GROUP + e
        for i in range(QB // CHUNK):
            w0 = (i // 2) * LANES
            rows = slice(e * QB + i * CHUNK, e * QB + (i + 1) * CHUNK)
            blocks = []
            for blk in range(CWIN // LANES):
                sb = s[rows, w0 + blk * LANES:w0 + (blk + 1) * LANES]
                if blk in _BIAS_BLOCKS[i % 2]:
                    sb = sb + bias_ref[head, i % 2, :, blk * LANES:(blk + 1) * LANES]
                blocks.append(sb)
            si = jnp.concatenate(blocks, axis=1)
            p = jnp.exp2(si - jnp.max(si, axis=-1, keepdims=True)).astype(BF16)
            pad = jnp.zeros((CHUNK, KWIN - CWIN), BF16)
            p_rows.append(jnp.concatenate([p, pad] if w0 == 0 else [pad, p], axis=1))
    return jnp.concatenate(p_rows, axis=0)


def _weighted_values(p, vbuf, qb, grp):
    o = _dot(p, vbuf[qb * QB:qb * QB + KWIN, grp * 2 * LANES:(grp + 1) * 2 * LANES])
    o = o[:, 0:LANES] / o[:, LANES:2 * LANES]
    return jnp.where(_head_mask(0), o[0:QB], o[QB:2 * QB])


def _attention(q_s, kbuf, vbuf, bias_ref, abuf, between):
    order = [(qb, grp) for qb in range(TS // QB) for grp in range(N_GROUPS)]
    s_next = _scores(q_s, kbuf, *order[0])
    for j, (qb, grp) in enumerate(order):
        s_cur = s_next
        if j + 1 < len(order):
            s_next = _scores(q_s, kbuf, *order[j + 1])
        p = _probs(s_cur, bias_ref, grp)
        abuf[qb * QB:(qb + 1) * QB, grp * LANES:(grp + 1) * LANES] = (
            _weighted_values(p, vbuf, qb, grp))
        between(j)


def _mixer_kernel(x_ref, w_in_ref, bias_ref, conv_w_ref, conv_b_ref,
                  w_rg_ref, b_rg_ref, w_ig_ref, b_ig_ref, lru_l_ref,
                  g_att_ref, g_lru_ref, w_out_ref,
                  wq_ref, wo_ref, w_gate_ref, w_up_ref, w_down_ref,
                  o_ref,
                  wq_bf_ref, wo_bf_ref, w_gate_bf_ref, w_up_bf_ref, w_down_bf_ref,
                  kbuf, vbuf, q_s, ubuf, gbuf, xc_s, xcb_s, pr_s, pi_s,
                  a_s, b_s, hbuf, abuf, m_s, hcar):
    t = pl.program_id(1)

    @pl.when(t == 0)
    def _():
        lane = lax.broadcasted_iota(jnp.int32, (LEFT + TS, 2 * D_ATT), 1)
        row = lax.broadcasted_iota(jnp.int32, (LEFT + TS, 2 * D_ATT), 0)
        const_lanes = (lane // LANES) % 2 == 1
        kbuf[...] = jnp.where(const_lanes & (row < LEFT), NEG, 0.0).astype(BF16)
        vbuf[...] = jnp.where(const_lanes, 1.0, 0.0).astype(BF16)
        ubuf[0:SUBLANES, :] = jnp.zeros((SUBLANES, D_LRU), F32)
        hcar[...] = jnp.zeros_like(hcar)

    x = x_ref[...]
    h = x.astype(BF16)
    rstd = lax.rsqrt(jnp.mean(x * x, axis=-1, keepdims=True) + EPS)
    ubuf[SUBLANES:SUBLANES + TS, :] = _dot(h, w_in_ref[:, 3 * D_ATT:3 * D_ATT + D_LRU]) * rstd
    gbuf[...] = _dot(h, w_in_ref[:, 3 * D_ATT + D_LRU:D_IN]) * rstd

    def proj_q():
        q_s[...] = (_dot(h, w_in_ref[:, 0:D_ATT]) * (rstd * (HEAD_DIM ** -0.5 * LOG2E))).astype(BF16)

    def proj_kv(buf, lo):
        f = (_dot(h, w_in_ref[:, lo:lo + D_ATT]) * rstd).astype(BF16)
        for grp in range(N_GROUPS):
            src = slice(grp * LANES, (grp + 1) * LANES)
            dst = slice(grp * 2 * LANES, grp * 2 * LANES + LANES)
            buf[LEFT:LEFT + TS, dst] = f[:, src]

    def rows_of(i):
        return slice(i * SLAB, (i + 1) * SLAB)

    def conv_slab(i):
        xc = conv_b_ref[...]
        for j in range(CONV_W):
            off = i * SLAB + SUBLANES - (CONV_W - 1) + j
            xc = xc + ubuf[off:off + SLAB, :] * conv_w_ref[j:j + 1, :]
        xc_s[rows_of(i), :] = xc
        xcb_s[rows_of(i), :] = xc.astype(BF16)

    def gate_piece(c):
        gate, half = divmod(c, 2)
        cols = slice(half * MXU_N, (half + 1) * MXU_N)
        w_ref, dst = ((w_rg_ref, pr_s), (w_ig_ref, pi_s))[gate]
        dst[:, cols] = _dot(xcb_s[:, cols], w_ref[half])

    neg_l = -lru_l_ref[...]
    decay_rate = (-LRU_C * LOG2E) * (jnp.maximum(neg_l, 0.0) + jnp.log1p(jnp.exp(-jnp.abs(neg_l))))

    def coef_slab(i):
        r = pl.reciprocal(1.0 + jnp.exp2(pr_s[rows_of(i), :] + b_rg_ref[...]))
        ig = pl.reciprocal(1.0 + jnp.exp2(pi_s[rows_of(i), :] + b_ig_ref[...]))
        a = jnp.exp2(decay_rate * r)
        z = jnp.maximum(1.0 - a * a, 0.0)
        mult = jnp.where(z > 0.0, z * lax.rsqrt(z), 0.0)
        a_s[rows_of(i), :] = a
        b_s[rows_of(i), :] = mult * (ig * xc_s[rows_of(i), :])

    def scan_slab(i, carry):
        groups = SLAB // SUBLANES
        a = a_s[rows_of(i), :].reshape(groups, SUBLANES, D_LRU)
        b = b_s[rows_of(i), :].reshape(groups, SUBLANES, D_LRU)
        row = lax.broadcasted_iota(jnp.int32, (1, SUBLANES, D_LRU), 1)
        for d in (1, 2, 4):
            keep = row >= d
            b = jnp.where(keep, a * pltpu.roll(b, d, axis=1) + b, b)
            a = jnp.where(keep, a * pltpu.roll(a, d, axis=1), a)
        for g in range(groups):
            hg = a[g] * carry + b[g]
            r0 = i * SLAB + g * SUBLANES
            hbuf[r0:r0 + SUBLANES, :] = hg
            carry = hg[SUBLANES - 1:SUBLANES, :]
        return carry

    def gated_slab(i):
        rec = hbuf[rows_of(i), :] * jax.nn.gelu(gbuf[rows_of(i), :])
        m_s[rows_of(i), D_ATT:D_MODEL] = _rmsnorm(rec, g_lru_ref[...]).astype(BF16)

    n_slabs = TS // SLAB
    state = {"carry": hcar[0:1, :]}

    def lru_slab(i):
        coef_slab(i)
        state["carry"] = scan_slab(i, state["carry"])
        gated_slab(i)

    proj_q()
    for i in range(n_slabs):
        conv_slab(i)
    for c in range(4):
        gate_piece(c)
    proj_kv(kbuf, D_ATT)
    for i in range(n_slabs // 2):
        lru_slab(i)
        if i == 0:
            proj_kv(vbuf, 2 * D_ATT)

    def between(j):
        if j < n_slabs - n_slabs // 2:
            lru_slab(n_slabs // 2 + j)

    _attention(q_s, kbuf, vbuf, bias_ref, abuf, between)
    hcar[0:1, :] = state["carry"]
    ubuf[0:SUBLANES, :] = ubuf[TS:TS + SUBLANES, :]
    kbuf[0:LEFT, :] = kbuf[TS:TS + LEFT, :]
    vbuf[0:LEFT, :] = vbuf[TS:TS + LEFT, :]

    m_s[:, 0:D_ATT] = _rmsnorm(abuf[...], g_att_ref[...]).astype(BF16)
    o_ref[...] = x + _dot(m_s[...], w_out_ref[...])

    for src, dst in ((wq_ref, wq_bf_ref), (wo_ref, wo_bf_ref), (w_gate_ref, w_gate_bf_ref),
                     (w_up_ref, w_up_bf_ref), (w_down_ref, w_down_bf_ref)):
        dst[...] = src[...].astype(BF16)


def _const_spec(shape):
    nd = len(shape)
    return pl.BlockSpec(shape, lambda *_: (0,) * nd, pipeline_mode=pl.Buffered(1))


def _cast_spec(shape, nt, steps):
    rows = next(r for r in range(16, shape[0] + 1, 16)
                if shape[0] % r == 0 and shape[0] // r <= steps)
    last = shape[0] // rows - 1
    return pl.BlockSpec((rows, shape[1]), lambda b, t: (jnp.minimum(b * nt + t, last), 0))


def _mixer(x, w_in, bias_tbl, conv_w, conv_b, w_rg, b_rg, w_ig, b_ig, lru_l,
           g_att, g_lru, w_out, next_weights):
    B, S, D = x.shape
    nt = S // TS
    tile = pl.BlockSpec((None, TS, D), lambda b, t: (b, t, 0))
    args = (w_in, bias_tbl, conv_w, conv_b, w_rg, b_rg, w_ig, b_ig, lru_l,
            g_att, g_lru, w_out)
    cast_specs = [_cast_spec(w.shape, nt, B * nt) for w in next_weights]
    kv = pltpu.VMEM((LEFT + TS, 2 * D_ATT), BF16)
    lru = pltpu.VMEM((TS, D_LRU), F32)
    return pl.pallas_call(
        _mixer_kernel,
        grid=(B, S // TS),
        in_specs=[tile] + [_const_spec(a.shape) for a in args] + cast_specs,
        out_specs=[tile] + cast_specs,
        out_shape=[jax.ShapeDtypeStruct((B, S, D), F32)]
                  + [jax.ShapeDtypeStruct(w.shape, BF16) for w in next_weights],
        scratch_shapes=[
            kv,
            kv,
            pltpu.VMEM((TS, D_ATT), BF16),
            pltpu.VMEM((SUBLANES + TS, D_LRU), F32),
            lru,
            lru,
            pltpu.VMEM((TS, D_LRU), BF16),
            lru, lru,
            lru, lru,
            lru,
            pltpu.VMEM((TS, D_ATT), F32),
            pltpu.VMEM((TS, D_MODEL), BF16),
            pltpu.VMEM((SUBLANES, D_LRU), F32),
        ],
        compiler_params=pltpu.CompilerParams(
            dimension_semantics=("arbitrary", "arbitrary"), vmem_limit_bytes=VMEM_LIMIT),
        name="mixer",
    )(x, *args, *next_weights)


def _cross_ffn_kernel(x_ref, kx_ref, vx_ref, g_cross_ref, wq_ref, wo_ref,
                      g_ffn_ref, w_gate_ref, w_up_ref, w_down_ref, g_final_ref, o_ref):
    def half(rows):
        x = x_ref[rows, :]
        hc = _rmsnorm(x, g_cross_ref[...]).astype(BF16)
        yield
        q = _dot(hc, wq_ref[...]).astype(BF16)
        yield

        def scores(hd):
            cols = slice(hd * X_HEAD_DIM, (hd + 1) * X_HEAD_DIM)
            return _dot_t(q[:, cols], kx_ref[:, cols])

        heads = []
        for hd in range(X_HEADS):
            cols = slice(hd * X_HEAD_DIM, (hd + 1) * X_HEAD_DIM)
            s = scores(hd)
            yield
            p = jnp.exp2(s - jnp.max(s, axis=-1, keepdims=True))
            l = jnp.sum(p, axis=-1, keepdims=True)
            yield
            heads.append((_dot(p.astype(BF16), vx_ref[:, cols]) / l).astype(BF16))
        x = x + _dot(jnp.concatenate(heads, axis=1), wo_ref[...])
        yield
        hf = _rmsnorm(x, g_ffn_ref[...]).astype(BF16)
        yield
        y = x
        for lo, hi in FF_SPLITS:
            gate = _dot(hf, w_gate_ref[:, lo:hi])
            up = _dot(hf, w_up_ref[:, lo:hi])
            yield
            act = (jax.nn.silu(gate) * up).astype(BF16)
            yield
            y = y + _dot(act, w_down_ref[lo:hi, :])
        yield
        o_ref[rows, :] = _rmsnorm(y, g_final_ref[...])

    n_half = 2
    rows_per = TM // n_half
    gens = [half(slice(k * rows_per, (k + 1) * rows_per)) for k in range(n_half)]
    next(gens[0])
    live = list(gens)
    while live:
        for g in list(live):
            try:
                next(g)
            except StopIteration:
                live.remove(g)


def _cross_ffn(x, kx, vx, g_cross, wq, wo, g_ffn, w_gate, w_up, w_down, g_final):
    B, S, D = x.shape
    M = kx.shape[1]
    tile = pl.BlockSpec((None, TM, D), lambda b, t: (b, t, 0))
    memspec = pl.BlockSpec((None, M, D), lambda b, t: (b, 0, 0))
    args = (g_cross, wq, wo, g_ffn, w_gate, w_up, w_down, g_final)
    return pl.pallas_call(
        _cross_ffn_kernel,
        grid=(B, S // TM),
        in_specs=[tile, memspec, memspec] + [_const_spec(a.shape) for a in args],
        out_specs=tile,
        out_shape=jax.ShapeDtypeStruct((B, S, D), F32),
        compiler_params=pltpu.CompilerParams(
            dimension_semantics=("arbitrary", "arbitrary"), vmem_limit_bytes=VMEM_LIMIT),
        name="cross_ffn",
    )(x, kx, vx, *args)


def _bias_table(rel_bias):
    n_heads = rel_bias.shape[0]
    rb = rel_bias.astype(F32) * LOG2E
    slot = np.arange(CWIN) // CHUNK
    tbl = []
    for parity in range(2):
        rel_max = (parity + LEFT_CHUNKS) * CHUNK + (CHUNK - 1)
        rel_min = (parity + LEFT_CHUNKS) * CHUNK - (CWIN - 1)
        hi, lo = min(rel_max, MAX_REL), max(rel_min, -MAX_REL)
        vec = jnp.concatenate([
            jnp.broadcast_to(rb[:, 2 * MAX_REL:], (n_heads, rel_max - hi)),
            jnp.flip(rb[:, lo + MAX_REL:hi + MAX_REL + 1], axis=1),
            jnp.broadcast_to(rb[:, :1], (n_heads, lo - rel_min)),
        ], axis=1)
        period = CWIN + 2 * CHUNK
        vecp = jnp.pad(vec, ((0, 0), (0, period - vec.shape[1])))
        flat = jnp.tile(vecp, (1, CHUNK))[:, :CHUNK * (period - 1)]
        rows = flat.reshape(n_heads, CHUNK, period - 1)[:, :, CHUNK - 1:CHUNK - 1 + CWIN]
        band = (slot >= parity) & (slot <= parity + LEFT_CHUNKS)
        far = rb[:, 2 * MAX_REL:, None]
        tbl.append(jnp.where(band[None, None, :], rows - far, NEG))
    return jnp.stack(tbl, axis=1)


def _block_diag(w):
    per = LRU_BLOCKS // 2
    halves = []
    for hlf in range(2):
        rows = []
        for i in range(per):
            blocks = [w[hlf * per + i] if j == i else jnp.zeros((LRU_BLOCK, LRU_BLOCK), w.dtype)
                      for j in range(per)]
            rows.append(jnp.concatenate(blocks, axis=1))
        halves.append(jnp.concatenate(rows, axis=0))
    return jnp.stack(halves).astype(BF16)


def kernel(x, mem, g_mix, w_in, rel_bias, conv_w, conv_b, w_rg, b_rg, w_ig, b_ig, lru_L,
           g_out_attn, g_out_lru, w_out, g_cross, g_mem, wq_c, wk_c, wv_c, wo_c,
           g_ffn, w_gate, w_up, w_down, g_final):
    depth = g_mix.shape[0]
    row = lambda v: v.reshape(1, -1)
    for l in range(depth):
        kx, vx, w_in_bf, w_out_bf = _mem_proj(mem, row(g_mem[l]), wk_c[l], wv_c[l], w_in[l],
                                              g_mix[l].reshape(-1, 1), w_out[l])
        x, wq_bf, wo_bf, w_gate_bf, w_up_bf, w_down_bf = _mixer(
            x, w_in_bf, _bias_table(rel_bias[l]),
            conv_w[l], row(conv_b[l]), _block_diag(-LOG2E * w_rg[l]), row(-LOG2E * b_rg[l]),
            _block_diag(-LOG2E * w_ig[l]), row(-LOG2E * b_ig[l]), row(lru_L[l]),
            row(g_out_attn[l]), row(g_out_lru[l]), w_out_bf,
            (wq_c[l], wo_c[l], w_gate[l], w_up[l], w_down[l]))
        assert depth == 1
        x = _cross_ffn(x, kx, vx, row(g_cross[l]), wq_bf, wo_bf, row(g_ffn[l]),
                       w_gate_bf, w_up_bf, w_down_bf, row(g_final))
    return x
```
